```python
import math
import jax, jax.numpy as jnp
from jax import lax
import numpy as np

D_MODEL = 2048
BATCH = 4
SEQ = 2048
DEPTH = 1

ATTN_HEADS = 8
HEAD_DIM = 128
ATTN_WIDTH = ATTN_HEADS * HEAD_DIM
REC_WIDTH = D_MODEL - ATTN_WIDTH
MIX_WIDTH = ATTN_WIDTH + REC_WIDTH
REC_BLOCKS = 8
REC_BLOCK = REC_WIDTH // REC_BLOCKS
CONV_WIDTH = 4
LRU_C = 8.0
IN_COLS = 3 * ATTN_WIDTH + 2 * REC_WIDTH
DILATED_PATTERNS = ((128, 1), (512, 4), (2048, 16))
ROPE_THETA = 10000.0
NEG_INF = -1e30
PEER_HEADS = 8
PEER_NKEYS = 128
PEER_EXPERTS = PEER_NKEYS * PEER_NKEYS
PEER_QDIM = 256
PEER_HALF = PEER_QDIM // 2
PEER_TOPK = 16
PEER_CHUNK = 128
DN_ALPHA = (2.0 * DEPTH) ** 0.25
DN_BETA = (8.0 * DEPTH) ** -0.25
LN_EPS = 1e-5

kernel_name = 'hymba_rglru_dilated_attn_peer'


def layer_norm(x, g, b):
    xf = x.astype(jnp.float32)
    mu = jnp.mean(xf, -1, keepdims=True)
    var = jnp.mean(jnp.square(xf - mu), -1, keepdims=True)
    return ((xf - mu) * lax.rsqrt(var + LN_EPS) * g.astype(jnp.float32) + b.astype(jnp.float32)).astype(x.dtype)


def rms_norm(x, g):
    xf = x.astype(jnp.float32)
    return xf * lax.rsqrt(jnp.mean(jnp.square(xf), -1, keepdims=True) + LN_EPS) * g.astype(jnp.float32)


def rope(x, positions):
    half = HEAD_DIM // 2
    inv = ROPE_THETA ** (-jnp.arange(half, dtype=jnp.float32) / half)
    ang = positions.astype(jnp.float32)[..., None] * inv
    cos = jnp.cos(ang)[:, :, None, :]
    sin = jnp.sin(ang)[:, :, None, :]
    x1, x2 = x[..., :half], x[..., half:]
    return jnp.concatenate([x1 * cos - x2 * sin, x2 * cos + x1 * sin], -1)


def banded_causal_attn(q, k, v, band):
    *lead, L, hd = q.shape
    nb = -(-L // band)
    lp = nb * band
    nlead = len(lead)
    qb = jnp.pad(q, [(0, 0)] * nlead + [(0, lp - L), (0, 0)]).reshape(*lead, nb, band, hd)
    kvpad = [(0, 0)] * nlead + [(band, lp - L), (0, 0)]
    kb = jnp.pad(k, kvpad).reshape(*lead, nb + 1, band, hd)
    vb = jnp.pad(v, kvpad).reshape(*lead, nb + 1, band, hd)
    kw = jnp.concatenate([kb[..., :-1, :, :], kb[..., 1:, :, :]], axis=-2)
    vw = jnp.concatenate([vb[..., :-1, :, :], vb[..., 1:, :, :]], axis=-2)
    s = jnp.einsum('...nqd,...nkd->...nqk', qb, kw)
    qi = jnp.arange(band)[:, None]
    kj = jnp.arange(2 * band)[None, :]
    dist = band + qi - kj
    kpos = jnp.arange(nb)[:, None, None] * band - band + kj
    valid = (dist >= 0) & (dist <= band) & (kpos >= 0)
    s = jnp.where(valid, s, NEG_INF)
    m = jnp.max(s, -1)
    p = jnp.exp(s - m[..., None])
    l = jnp.sum(p, -1)
    acc = jnp.einsum('...nqk,...nkd->...nqd', p, vw)
    return (acc.reshape(*lead, lp, hd)[..., :L, :],
            m.reshape(*lead, lp)[..., :L],
            l.reshape(*lead, lp)[..., :L])


def dilated_attention(q, k, v):
    B, S, H, hd = q.shape
    accs, ms, ls = [], [], []
    for window, dil in DILATED_PATTERNS:
        L = S // dil
        def split(t):
            return t.reshape(B, L, dil, H, hd).transpose(0, 2, 3, 1, 4)
        acc, m, l = banded_causal_attn(split(q), split(k), split(v), window // dil)
        accs.append(acc.transpose(0, 3, 1, 2, 4).reshape(B, S, H, hd))
        ms.append(m.transpose(0, 3, 1, 2).reshape(B, S, H))
        ls.append(l.transpose(0, 3, 1, 2).reshape(B, S, H))
    m_all = jnp.stack(ms)
    wts = jnp.exp(m_all - jnp.max(m_all, 0))
    num = jnp.einsum('pbsh,pbshd->bshd', wts, jnp.stack(accs))
    den = jnp.sum(wts * jnp.stack(ls), 0)
    return num / den[..., None]


def causal_depthwise_conv(x, w, bias):
    C = x.shape[-1]
    y = lax.conv_general_dilated(x, w.astype(x.dtype)[:, None, :], window_strides=(1,),
                                 padding=[(CONV_WIDTH - 1, 0)],
                                 dimension_numbers=('NWC', 'WIO', 'NWC'),
                                 feature_group_count=C)
    return y + bias.astype(x.dtype)


def rg_lru(x, w_a, b_a, w_x, b_x, lam):
    B, S, R = x.shape
    xb = x.reshape(B, S, REC_BLOCKS, REC_BLOCK)
    r = jax.nn.sigmoid(jnp.einsum('bsnc,ncd->bsnd', xb, w_a.astype(jnp.float32)).reshape(B, S, R) + b_a.astype(jnp.float32))
    i = jax.nn.sigmoid(jnp.einsum('bsnc,ncd->bsnd', xb, w_x.astype(jnp.float32)).reshape(B, S, R) + b_x.astype(jnp.float32))
    log_a = -LRU_C * r * jax.nn.softplus(-lam.astype(jnp.float32))
    a = jnp.exp(log_a)
    bterm = jnp.sqrt(-jnp.expm1(2.0 * log_a)) * (i * x)
    def combine(left, right):
        a1, b1 = left
        a2, b2 = right
        return a1 * a2, a2 * b1 + b2
    _, h = lax.associative_scan(combine, (a, bterm), axis=1)
    return h


def peer(x, wq, keys1, keys2, u, v):
    B, S, D = x.shape
    T = B * S
    xt = x.reshape(T, D)
    q = (xt @ wq).astype(jnp.float32).reshape(T, PEER_HEADS, 2, PEER_HALF)
    s1 = jnp.einsum('thd,kd->thk', q[:, :, 0], keys1.astype(jnp.float32))
    s2 = jnp.einsum('thd,kd->thk', q[:, :, 1], keys2.astype(jnp.float32))
    v1, i1 = lax.top_k(s1, PEER_TOPK)
    v2, i2 = lax.top_k(s2, PEER_TOPK)
    cand_s = (v1[..., :, None] + v2[..., None, :]).reshape(T, PEER_HEADS, PEER_TOPK * PEER_TOPK)
    cand_i = (i1[..., :, None] * PEER_NKEYS + i2[..., None, :]).reshape(T, PEER_HEADS, PEER_TOPK * PEER_TOPK)
    top_s, pos = lax.top_k(cand_s, PEER_TOPK)
    idx = jnp.take_along_axis(cand_i, pos, axis=-1).reshape(T, PEER_HEADS * PEER_TOPK)
    gates = jax.nn.softmax(top_s, axis=-1).reshape(T, PEER_HEADS * PEER_TOPK).astype(x.dtype)
    n_chunks = T // PEER_CHUNK
    E = PEER_HEADS * PEER_TOPK
    def chunk_fn(args):
        xc, ic, gc = args
        uc = jnp.take(u, ic, axis=0)
        act = jax.nn.gelu(jnp.einsum('cd,ced->ce', xc, uc))
        vc = jnp.take(v, ic, axis=0)
        return jnp.einsum('ce,ced->cd', gc * act, vc)
    out = lax.map(chunk_fn, (xt.reshape(n_chunks, PEER_CHUNK, D),
                             idx.reshape(n_chunks, PEER_CHUNK, E),
                             gates.reshape(n_chunks, PEER_CHUNK, E)))
    return out.reshape(B, S, D)


def setup_inputs(seed: int = 0) -> dict:
    key = jax.random.key(seed)
    ks = jax.random.split(key, 24)
    f32 = jnp.float32
    nrm = lambda k, shape, s: jax.random.normal(k, shape, f32) * s
    x = jax.random.normal(ks[0], (BATCH, SEQ, D_MODEL), f32)
    positions = jnp.broadcast_to(jnp.arange(SEQ, dtype=jnp.int32), (BATCH, SEQ))
    col_scale = jnp.concatenate([jnp.ones((2 * ATTN_WIDTH,), f32),
                                 jnp.full((ATTN_WIDTH,), DN_BETA, f32),
                                 jnp.ones((2 * REC_WIDTH,), f32)])
    w_in = nrm(ks[1], (DEPTH, D_MODEL, IN_COLS), D_MODEL ** -0.5) * col_scale
    conv_w = nrm(ks[2], (DEPTH, CONV_WIDTH, REC_WIDTH), CONV_WIDTH ** -0.5)
    conv_b = nrm(ks[3], (DEPTH, REC_WIDTH), 0.01)
    rg_w_a = nrm(ks[4], (DEPTH, REC_BLOCKS, REC_BLOCK, REC_BLOCK), REC_BLOCK ** -0.5)
    rg_b_a = nrm(ks[5], (DEPTH, REC_WIDTH), 0.01)
    rg_w_x = nrm(ks[6], (DEPTH, REC_BLOCKS, REC_BLOCK, REC_BLOCK), REC_BLOCK ** -0.5)
    rg_b_x = nrm(ks[7], (DEPTH, REC_WIDTH), 0.01)
    a_c = jax.random.uniform(ks[8], (DEPTH, REC_WIDTH), f32, 0.9, 0.999)
    a0 = a_c ** (1.0 / LRU_C)
    lru_lambda = jnp.log(a0) - jnp.log1p(-a0)
    gn_attn = 1.0 + nrm(ks[9], (DEPTH, ATTN_WIDTH), 0.01)
    gn_rec = 1.0 + nrm(ks[10], (DEPTH, REC_WIDTH), 0.01)
    w_out = nrm(ks[11], (DEPTH, MIX_WIDTH, D_MODEL), MIX_WIDTH ** -0.5 * DN_BETA)
    ln1_g = 1.0 + nrm(ks[12], (DEPTH, D_MODEL), 0.01)
    ln1_b = nrm(ks[13], (DEPTH, D_MODEL), 0.01)
    peer_wq = nrm(ks[14], (DEPTH, D_MODEL, PEER_HEADS * PEER_QDIM), D_MODEL ** -0.5)
    peer_keys1 = nrm(ks[15], (DEPTH, PEER_NKEYS, PEER_HALF), PEER_HALF ** -0.5)
    peer_keys2 = nrm(ks[16], (DEPTH, PEER_NKEYS, PEER_HALF), PEER_HALF ** -0.5)
    peer_u = nrm(ks[17], (DEPTH, PEER_EXPERTS, D_MODEL), D_MODEL ** -0.5 * DN_BETA)
    peer_v = nrm(ks[18], (DEPTH, PEER_EXPERTS, D_MODEL), DN_BETA)
    ln2_g = 1.0 + nrm(ks[19], (DEPTH, D_MODEL), 0.01)
    ln2_b = nrm(ks[20], (DEPTH, D_MODEL), 0.01)
    return {'x': x, 'positions': positions, 'w_in': w_in, 'conv_w': conv_w, 'conv_b': conv_b,
            'rg_w_a': rg_w_a, 'rg_b_a': rg_b_a, 'rg_w_x': rg_w_x, 'rg_b_x': rg_b_x,
            'lru_lambda': lru_lambda, 'gn_attn': gn_attn, 'gn_rec': gn_rec, 'w_out': w_out,
            'ln1_g': ln1_g, 'ln1_b': ln1_b, 'peer_wq': peer_wq, 'peer_keys1': peer_keys1,
            'peer_keys2': peer_keys2, 'peer_u': peer_u, 'peer_v': peer_v,
            'ln2_g': ln2_g, 'ln2_b': ln2_b}


def reference(x, positions, w_in, conv_w, conv_b, rg_w_a, rg_b_a, rg_w_x, rg_b_x, lru_lambda,
              gn_attn, gn_rec, w_out, ln1_g, ln1_b, peer_wq, peer_keys1, peer_keys2,
              peer_u, peer_v, ln2_g, ln2_b):
    B, S, _ = x.shape
    h = x
    for l in range(DEPTH):
        proj = h @ w_in[l]
        q, k, v, xr, gr = jnp.split(proj, [ATTN_WIDTH, 2 * ATTN_WIDTH, 3 * ATTN_WIDTH,
                                           3 * ATTN_WIDTH + REC_WIDTH], axis=-1)
        q = rope(q.astype(jnp.float32).reshape(B, S, ATTN_HEADS, HEAD_DIM), positions) * (HEAD_DIM ** -0.5)
        k = rope(k.astype(jnp.float32).reshape(B, S, ATTN_HEADS, HEAD_DIM), positions)
        v = v.astype(jnp.float32).reshape(B, S, ATTN_HEADS, HEAD_DIM)
        attn = dilated_attention(q, k, v).reshape(B, S, ATTN_WIDTH)
        xc = causal_depthwise_conv(xr, conv_w[l], conv_b[l]).astype(jnp.float32)
        rec = rg_lru(xc, rg_w_a[l], rg_b_a[l], rg_w_x[l], rg_b_x[l], lru_lambda[l]) \
            * jax.nn.gelu(gr.astype(jnp.float32))
        mix = jnp.concatenate([rms_norm(attn, gn_attn[l]), rms_norm(rec, gn_rec[l])], -1).astype(h.dtype)
        h = layer_norm(DN_ALPHA * h + mix @ w_out[l], ln1_g[l], ln1_b[l])
        y = peer(h, peer_wq[l], peer_keys1[l], peer_keys2[l], peer_u[l], peer_v[l])
        h = layer_norm(DN_ALPHA * h + y, ln2_g[l], ln2_b[l])
    return h
```

```python
import functools
import math

import jax
import jax.numpy as jnp
import numpy as np
from jax import lax
from jax.experimental import pallas as pl
from jax.experimental.pallas import tpu as pltpu

F32 = jnp.float32
BF16 = jnp.bfloat16

ATTN_HEADS = 8
HEAD_DIM = 128
ATTN_WIDTH = ATTN_HEADS * HEAD_DIM
REC_BLOCKS = 8
REC_BLOCK = 128
REC_WIDTH = REC_BLOCKS * REC_BLOCK
CONV_WIDTH = 4
LRU_C = 8.0
BAND = 128
DILATIONS = (1, 4, 16)
ROPE_THETA = 10000.0
NEG_INF = -1e30
PEER_HEADS = 8
PEER_NKEYS = 128
PEER_EXPERTS = PEER_NKEYS * PEER_NKEYS
PEER_HALF = 128
PEER_TOPK = 16
SUBLANES = 8

V7X_VMEM_LIMIT = 56 * 1024 * 1024


def _params(sem, vmem=V7X_VMEM_LIMIT):
    return pltpu.CompilerParams(dimension_semantics=sem, vmem_limit_bytes=vmem)


def _resident(shape):
    zeros = (0,) * len(shape)
    return pl.BlockSpec(shape, lambda *_: zeros, pipeline_mode=pl.Buffered(1))


def _exact(v):
    assert float(np.float32(v)) == float(v)
    return jnp.full((1, 1), v, F32)


_PI_HI = float(np.float32(math.pi))
_PI_MID = float(np.float32(math.pi - _PI_HI))
_PI_LO = float(np.float32(math.pi - _PI_HI - _PI_MID))


def _ln_eps():
    return _exact(1.0) / _exact(1e5)


def _gelu_tanh(x):
    pi = _exact(_PI_HI) + _exact(_PI_MID) + _exact(_PI_LO)
    c = jnp.sqrt(_exact(2.0) / pi)
    k = _exact(44715.0) / _exact(1e6)
    return 0.5 * x * (1.0 + jnp.tanh(c * (x + k * (x * x * x))))


def _sigmoid(x):
    return 1.0 / (1.0 + jnp.exp(-x))


def _rms_scale(x):
    return lax.rsqrt(jnp.mean(x * x, axis=-1, keepdims=True) + _ln_eps())


def _layer_norm(z, g, b):
    mu = jnp.mean(z, axis=-1, keepdims=True)
    zc = z - mu
    var = jnp.mean(zc * zc, axis=-1, keepdims=True)
    return zc * lax.rsqrt(var + _ln_eps()) * g + b


def _deepnorm_alpha(depth):
    return jnp.sqrt(jnp.sqrt(_exact(2.0 * depth)))


QKV_TILE = 512


def _qkv_kernel(x_ref, pos_ref, inv_ref, w_ref, q_ref, k_ref, v_ref):
    proj = jnp.dot(x_ref[0].astype(BF16), w_ref[...], preferred_element_type=F32)
    ang = pos_ref[0].astype(F32) * inv_ref[...]
    lane = lax.broadcasted_iota(jnp.int32, ang.shape, 1)
    cos = jnp.cos(ang)
    sin = jnp.where(lane < HEAD_DIM // 2, -jnp.sin(ang), jnp.sin(ang))
    scale = lax.rsqrt(_exact(float(HEAD_DIM)))
    for h in range(ATTN_HEADS):
        cols = slice(h * HEAD_DIM, (h + 1) * HEAD_DIM)
        qh = proj[:, cols]
        kh = proj[:, ATTN_WIDTH + h * HEAD_DIM: ATTN_WIDTH + (h + 1) * HEAD_DIM]
        q_ref[0, :, cols] = (qh * cos + pltpu.roll(qh, HEAD_DIM // 2, 1) * sin) * scale
        k_ref[0, :, cols] = kh * cos + pltpu.roll(kh, HEAD_DIM // 2, 1) * sin
    v_ref[0] = proj[:, 2 * ATTN_WIDTH:]


def _qkv_call(x, pos, inv_full, w_qkv):
    B, S, D = x.shape
    tm = QKV_TILE
    spec = pl.BlockSpec((1, tm, ATTN_WIDTH), lambda b, s: (b, s, 0))
    shape = jax.ShapeDtypeStruct((B, S, ATTN_WIDTH), F32)
    return pl.pallas_call(
        _qkv_kernel,
        grid=(B, S // tm),
        in_specs=[pl.BlockSpec((1, tm, D), lambda b, s: (b, s, 0)),
                  pl.BlockSpec((1, tm, 1), lambda b, s: (b, s, 0)),
                  _resident((1, HEAD_DIM)),
                  _resident(w_qkv.shape)],
        out_specs=[spec, spec, spec],
        out_shape=[shape, shape, shape],
        compiler_params=_params(("parallel", "parallel")),
        name="qkv_rope",
    )(x, pos, inv_full, w_qkv)


REC_TILE = 512


def _rec_kernel(x_ref, w_ref, cw_ref, cb_ref, wax_ref, ba_ref, bx_ref, lam_ref, g_ref, o_ref,
                ext_ref, a_ref, b_ref, h_ref, hc_ref):
    tm = REC_TILE

    @pl.when(pl.program_id(1) == 0)
    def _():
        ext_ref[0:SUBLANES, :] = jnp.zeros((SUBLANES, REC_WIDTH), F32)
        hc_ref[...] = jnp.zeros((SUBLANES, REC_WIDTH), F32)

    proj = jnp.dot(x_ref[0].astype(BF16), w_ref[...], preferred_element_type=F32)
    xr = proj[:, :REC_WIDTH]
    gr = proj[:, REC_WIDTH:]
    ext_ref[SUBLANES:SUBLANES + tm, :] = xr
    xc = cb_ref[...] + cw_ref[CONV_WIDTH - 1:CONV_WIDTH, :] * xr
    for j in range(CONV_WIDTH - 1):
        back = CONV_WIDTH - 1 - j
        xc = xc + cw_ref[j:j + 1, :] * ext_ref[pl.ds(SUBLANES - back, tm), :]
    ext_ref[0:SUBLANES, :] = xr[tm - SUBLANES:, :]

    rs, is_ = [], []
    for n in range(REC_BLOCKS):
        xb = xc[:, n * REC_BLOCK:(n + 1) * REC_BLOCK].astype(BF16)
        gts = jnp.dot(xb, wax_ref[n], preferred_element_type=F32)
        rs.append(gts[:, :REC_BLOCK])
        is_.append(gts[:, REC_BLOCK:])
    r = _sigmoid(jnp.concatenate(rs, axis=1) + ba_ref[...])
    ig = _sigmoid(jnp.concatenate(is_, axis=1) + bx_ref[...])
    z = -lam_ref[...]
    softplus = jnp.maximum(z, 0.0) + jnp.log(1.0 + jnp.exp(-jnp.abs(z)))
    log_a = (-LRU_C) * r * softplus
    a = jnp.exp(log_a)
    bt = jnp.sqrt(1.0 - jnp.exp(2.0 * log_a)) * (ig * xc)

    r8 = lax.broadcasted_iota(jnp.int32, a.shape, 0) & (SUBLANES - 1)
    for s in (1, 2, 4):
        a_sh = pltpu.roll(a, s, 0)
        b_sh = pltpu.roll(bt, s, 0)
        m = r8 >= s
        bt = jnp.where(m, a * b_sh + bt, bt)
        a = jnp.where(m, a * a_sh, a)
    a_ref[...] = a
    b_ref[...] = bt

    def group(g, hc):
        rows = pl.ds(pl.multiple_of(g * SUBLANES, SUBLANES), SUBLANES)
        h = a_ref[rows, :] * hc + b_ref[rows, :]
        h_ref[rows, :] = h
        return jnp.broadcast_to(h[SUBLANES - 1:SUBLANES, :], (SUBLANES, REC_WIDTH))

    hc_ref[...] = lax.fori_loop(0, tm // SUBLANES, group, hc_ref[...], unroll=8)

    rec = h_ref[...] * _gelu_tanh(gr)
    o_ref[0] = (rec * _rms_scale(rec) * g_ref[...]).astype(BF16)


def _rec_call(x, w_rec, conv_w, conv_b, wax, b_a, b_x, lam, gn_rec):
    B, S, D = x.shape
    tm = REC_TILE
    row = lambda a: a.reshape(1, REC_WIDTH)
    return pl.pallas_call(
        _rec_kernel,
        grid=(B, S // tm),
        in_specs=[pl.BlockSpec((1, tm, D), lambda b, s: (b, s, 0)),
                  _resident(w_rec.shape),
                  _resident((CONV_WIDTH, REC_WIDTH)),
                  _resident((1, REC_WIDTH)),
                  _resident(wax.shape),
                  _resident((1, REC_WIDTH)), _resident((1, REC_WIDTH)),
                  _resident((1, REC_WIDTH)), _resident((1, REC_WIDTH))],
        out_specs=pl.BlockSpec((1, tm, REC_WIDTH), lambda b, s: (b, s, 0)),
        out_shape=jax.ShapeDtypeStruct((B, S, REC_WIDTH), BF16),
        scratch_shapes=[pltpu.VMEM((SUBLANES + tm, REC_WIDTH), F32),
                        pltpu.VMEM((tm, REC_WIDTH), F32),
                        pltpu.VMEM((tm, REC_WIDTH), F32),
                        pltpu.VMEM((tm, REC_WIDTH), F32),
                        pltpu.VMEM((SUBLANES, REC_WIDTH), F32)],
        compiler_params=_params(("parallel", "arbitrary")),
        name="rec_branch",
    )(x, w_rec, conv_w, row(conv_b), wax, row(b_a), row(b_x), row(lam), row(gn_rec))


ATTN_UNROLL = 8


def _band_block(q, k, v, valid):
    s = lax.dot_general(q.astype(BF16), k.astype(BF16), (((1,), (1,)), ((), ())), preferred_element_type=F32)
    s = jnp.where(valid, s, NEG_INF)
    m = jnp.max(s, axis=-1, keepdims=True)
    p = jnp.exp(s - m)
    l = jnp.sum(p, axis=-1, keepdims=True)
    acc = jnp.dot(p.astype(BF16), v.astype(BF16), preferred_element_type=F32)
    return acc, m, l


def _attn_kernel(q_ref, k_ref, v_ref, o_ref, *stats, seq):
    wide = (BAND, HEAD_DIM)

    for p, dil in enumerate(DILATIONS):
        acc_ref, m_ref, l_ref = stats[3 * p: 3 * p + 3]
        n_blk = seq // (dil * BAND)

        def block(idx, carry, dil=dil, n_blk=n_blk, acc_ref=acc_ref, m_ref=m_ref, l_ref=l_ref):
            c = idx // n_blk
            j = idx - c * n_blk
            k0 = jnp.maximum(j - 1, 0) * BAND
            n_keys = 2 * BAND if n_blk > 1 else BAND
            if dil == 1:
                rows = pl.ds(pl.multiple_of(j * BAND, BAND), BAND)
                krows = pl.ds(pl.multiple_of(k0, BAND), n_keys)
            else:
                rows = pl.ds(dil * j * BAND + c, BAND, stride=dil)
                krows = pl.ds(dil * k0 + c, n_keys, stride=dil)
            qi = lax.broadcasted_iota(jnp.int32, (BAND, n_keys), 0)
            kj = lax.broadcasted_iota(jnp.int32, (BAND, n_keys), 1)
            dist = (j * BAND + qi) - (k0 + kj)
            acc, m, l = _band_block(q_ref[0, rows, :], k_ref[0, krows, :], v_ref[0, krows, :],
                                    (dist >= 0) & (dist <= BAND))
            acc_ref[rows, :] = acc
            m_ref[rows, :] = jnp.broadcast_to(m, wide)
            l_ref[rows, :] = jnp.broadcast_to(l, wide)
            return carry

        lax.fori_loop(0, seq // BAND, block, 0, unroll=ATTN_UNROLL)

    def merge(j, carry):
        rows = pl.ds(pl.multiple_of(j * BAND, BAND), BAND)
        ms = [stats[3 * p + 1][rows, :] for p in range(3)]
        mx = jnp.maximum(jnp.maximum(ms[0], ms[1]), ms[2])
        ws = [jnp.exp(mm - mx) for mm in ms]
        num = sum(ws[p] * stats[3 * p][rows, :] for p in range(3))
        den = sum(ws[p] * stats[3 * p + 2][rows, :] for p in range(3))
        o_ref[0, rows, :] = num / den
        return carry

    lax.fori_loop(0, seq // BAND, merge, 0, unroll=ATTN_UNROLL)


def _attn_call(q, k, v):
    B, S, _ = q.shape
    assert S % (max(DILATIONS) * BAND) == 0
    spec = pl.BlockSpec((1, S, HEAD_DIM), lambda b, h: (b, 0, h))
    stat = pltpu.VMEM((S, HEAD_DIM), F32)
    return pl.pallas_call(
        functools.partial(_attn_kernel, seq=S),
        grid=(B, ATTN_HEADS),
        in_specs=[spec] * 3,
        out_specs=spec,
        out_shape=jax.ShapeDtypeStruct((B, S, ATTN_WIDTH), F32),
        scratch_shapes=[stat] * 9,
        compiler_params=_params(("parallel", "parallel")),
        name="dilated_attn",
    )(q, k, v)


MIX_TILE = 512


def _mix_kernel(x_ref, at_ref, rc_ref, wo_ref, ga_ref, g_ref, b_ref, o_ref, ob_ref, *, depth):
    at = at_ref[0]
    at_n = (at * _rms_scale(at) * ga_ref[...]).astype(BF16)
    y = jnp.dot(at_n, wo_ref[:ATTN_WIDTH, :], preferred_element_type=F32)
    y = y + jnp.dot(rc_ref[0], wo_ref[ATTN_WIDTH:, :], preferred_element_type=F32)
    h = _layer_norm(_deepnorm_alpha(depth) * x_ref[0] + y, g_ref[...], b_ref[...])
    o_ref[0] = h
    ob_ref[0] = h.astype(BF16)


def _mix_call(x, attn, rec_n, w_out, gn_attn, ln_g, ln_b, depth):
    B, S, D = x.shape
    tm = MIX_TILE
    row = lambda a, n: a.reshape(1, n)
    tile = lambda n: pl.BlockSpec((1, tm, n), lambda b, s: (b, s, 0))
    return pl.pallas_call(
        functools.partial(_mix_kernel, depth=depth),
        grid=(B, S // tm),
        in_specs=[tile(D), tile(ATTN_WIDTH), tile(REC_WIDTH),
                  _resident(w_out.shape),
                  _resident((1, ATTN_WIDTH)), _resident((1, D)), _resident((1, D))],
        out_specs=[tile(D), tile(D)],
        out_shape=[jax.ShapeDtypeStruct((B, S, D), F32), jax.ShapeDtypeStruct((B, S, D), BF16)],
        compiler_params=_params(("parallel", "parallel")),
        name="mix_out_ln",
    )(x, attn, rec_n, w_out, row(gn_attn, ATTN_WIDTH), row(ln_g, D), row(ln_b, D))


ROUTER_TILE = 256
N_CAND_ROWS = 16 + 7 * 8 + 8


def _top16_rows(s, exact):
    n = s.shape[0]
    idx = lax.broadcasted_iota(jnp.int32, s.shape, 0)
    rank = jnp.full(s.shape, float(PEER_TOPK), F32)
    vals = []
    for r in range(PEER_TOPK):
        m = jnp.max(s, axis=0, keepdims=True)
        sel = s == m
        if exact:
            first = jnp.min(jnp.where(sel, idx, n), axis=0, keepdims=True)
            sel = idx == first
        rank = jnp.where(sel, float(r), rank)
        s = jnp.where(sel, -jnp.inf, s)
        vals.append(m)
    ranked = jnp.sum(jnp.where(rank < PEER_TOPK, 1.0, 0.0), axis=0, keepdims=True)
    return rank, jnp.concatenate(vals, axis=0), ranked


def _route_head(s1, s2, exact):
    rank1, v1, n1 = _top16_rows(s1, exact)
    rank2, v2, n2 = _top16_rows(s2, exact)
    blocks = [v1[0:1] + v2]
    for r1 in range(1, 8):
        blocks.append(v1[r1:r1 + 1] + v2[0:8])
    blocks.append(v1[8:16] + v2[0:1])
    cand = jnp.concatenate(blocks, axis=0)
    top = cand[0:1]
    idx = lax.broadcasted_iota(jnp.int32, cand.shape, 0)
    work = cand
    chosen = jnp.zeros(cand.shape, F32)
    for _ in range(PEER_TOPK):
        m = jnp.max(work, axis=0, keepdims=True)
        sel = work == m
        if exact:
            first = jnp.min(jnp.where(sel, idx, N_CAND_ROWS), axis=0, keepdims=True)
            sel = idx == first
        chosen = jnp.where(sel, 1.0, chosen)
        work = jnp.where(sel, -jnp.inf, work)
    n3 = jnp.sum(chosen, axis=0, keepdims=True)
    z = jnp.sum(chosen * jnp.exp(cand - top), axis=0, keepdims=True)
    counts = [jnp.sum(chosen[0:16], axis=0, keepdims=True)]
    for r1 in range(1, 8):
        counts.append(jnp.sum(chosen[8 + 8 * r1: 16 + 8 * r1], axis=0, keepdims=True))
    counts.append(chosen[N_CAND_ROWS - 8:])
    cnt_r = jnp.concatenate(counts, axis=0)
    cnt_k = jnp.zeros(s1.shape, F32)
    for r in range(PEER_TOPK):
        cnt_k = jnp.where(rank1 == r, cnt_r[r:r + 1], cnt_k)
    clean = (n1 == PEER_TOPK) & (n2 == PEER_TOPK) & (n3 == PEER_TOPK)
    ties = jnp.max(jnp.where(clean, 0.0, 1.0))
    return rank2, jnp.exp(s2 - v2[0:1]), jnp.exp(s1 - v1[0:1]) / z, cnt_k, ties


def _router_kernel(h_ref, wq_ref, k1_ref, k2_ref, rank2_ref, e2_ref, coef_ref, cnt_ref):
    hb = h_ref[...]
    nt = (((1,), (1,)), ((), ()))

    def head(h, carry):
        q = jnp.dot(hb, wq_ref[h], preferred_element_type=F32)
        s1 = lax.dot_general(k1_ref[...], q[:, :PEER_HALF].astype(BF16), nt, preferred_element_type=F32)
        s2 = lax.dot_general(k2_ref[...], q[:, PEER_HALF:].astype(BF16), nt, preferred_element_type=F32)

        def emit(exact):
            rank2, e2, coef, cnt_k, ties = _route_head(s1, s2, exact)
            rank2_ref[h] = rank2.astype(BF16)
            e2_ref[h] = e2.astype(BF16)
            coef_ref[h] = coef
            cnt_ref[h] = cnt_k
            return ties

        ties = emit(False)

        @pl.when(ties > 0.0)
        def _():
            emit(True)

        return carry

    lax.fori_loop(0, PEER_HEADS, head, 0)


def _router_call(hb, wq_h, k1, k2):
    T, D = hb.shape
    tm = ROUTER_TILE
    o_spec = pl.BlockSpec((PEER_HEADS, PEER_NKEYS, tm), lambda t: (0, 0, t))
    shape = lambda dt: jax.ShapeDtypeStruct((PEER_HEADS, PEER_NKEYS, T), dt)
    return pl.pallas_call(
        _router_kernel,
        grid=(T // tm,),
        in_specs=[pl.BlockSpec((tm, D), lambda t: (t, 0)),
                  _resident(wq_h.shape), _resident(k1.shape), _resident(k2.shape)],
        out_specs=[o_spec] * 4,
        out_shape=[shape(BF16), shape(BF16), shape(F32), shape(F32)],
        compiler_params=_params(("parallel",)),
        name="peer_router",
    )(hb, wq_h, k1, k2)


PREP_TILE = 512


def _uvprep_kernel(u_ref, v_ref, ub_ref, vt_ref):
    ub_ref[...] = u_ref[...].astype(BF16)
    vt_ref[...] = v_ref[...].T.astype(BF16)


def _uvprep_call(u, v):
    E, D = u.shape
    te = PREP_TILE
    return pl.pallas_call(
        _uvprep_kernel,
        grid=(E // te,),
        in_specs=[pl.BlockSpec((te, D), lambda e: (e, 0)), pl.BlockSpec((te, D), lambda e: (e, 0))],
        out_specs=[pl.BlockSpec((te, D), lambda e: (e, 0)), pl.BlockSpec((D, te), lambda e: (0, e))],
        out_shape=[jax.ShapeDtypeStruct((E, D), BF16), jax.ShapeDtypeStruct((D, E), BF16)],
        compiler_params=_params(("parallel",)),
        name="peer_uvprep",
    )(u, v)


PEER_TOK_TILE = 512
PEER_EXP_TILE = 1024
KEYS_PER_STEP = PEER_EXP_TILE // PEER_NKEYS
PEER_STAGES = 3


def _peer_kernel(h_ref, u_ref, vt_ref, rank2_ref, e2_ref, coef_ref, cnt_ref, o_ref, act_ref, p_ref, *, n_exp_tiles):
    s = pl.program_id(0)
    cur = s % 2
    prev = (s + 1) % 2

    @pl.when(s == 0)
    def _():
        act_ref[1] = jnp.zeros(act_ref.shape[1:], F32)
        p_ref[0] = jnp.zeros(p_ref.shape[1:], BF16)

    @pl.when(jnp.maximum(s - (PEER_STAGES - 1), 0) % n_exp_tiles == 0)
    def _():
        o_ref[...] = jnp.zeros(o_ref.shape, F32)

    act_ref[cur] = lax.dot_general(u_ref[...], h_ref[...], (((1,), (1,)), ((), ())), preferred_element_type=F32)

    for key in range(KEYS_PER_STEP):
        gate = None
        for h in range(PEER_HEADS):
            cnt = cnt_ref[h, key:key + 1, :].astype(BF16)
            cf = coef_ref[h, key:key + 1, :].astype(BF16)
            term = jnp.where(rank2_ref[h] < cnt, e2_ref[h], jnp.zeros((), BF16)) * cf
            gate = term if gate is None else gate + term
        rows = slice(key * PEER_NKEYS, (key + 1) * PEER_NKEYS)
        p_ref[prev, rows, :] = gate * _gelu_tanh(act_ref[prev, rows, :]).astype(BF16)

    o_ref[...] += jnp.dot(vt_ref[...], p_ref[cur], preferred_element_type=F32)


def _peer_call(hb, ub, vt, rank2, e2, coef, cnt):
    T, D = hb.shape
    tm, te = PEER_TOK_TILE, PEER_EXP_TILE
    n_e = PEER_EXPERTS // te
    n_pairs = (T // tm) * n_e
    by_tile = lambda a: a.reshape(PEER_HEADS, n_e, KEYS_PER_STEP, T)

    def pair(s, lag):
        i = jnp.clip(s - lag, 0, n_pairs - 1)
        return i // n_e, i % n_e

    head_spec = pl.BlockSpec((PEER_HEADS, PEER_NKEYS, tm), lambda s: (0, 0, pair(s, 1)[0]))
    key_spec = pl.BlockSpec((PEER_HEADS, None, KEYS_PER_STEP, tm),
                            lambda s: (0, pair(s, 1)[1], 0, pair(s, 1)[0]))
    return pl.pallas_call(
        functools.partial(_peer_kernel, n_exp_tiles=n_e),
        grid=(n_pairs + PEER_STAGES - 1,),
        in_specs=[pl.BlockSpec((tm, D), lambda s: (pair(s, 0)[0], 0)),
                  pl.BlockSpec((te, D), lambda s: (pair(s, 0)[1], 0)),
                  pl.BlockSpec((D, te), lambda s: (0, pair(s, 2)[1])),
                  head_spec, head_spec, key_spec, key_spec],
        out_specs=pl.BlockSpec((D, tm), lambda s: (0, pair(s, 2)[0])),
        out_shape=jax.ShapeDtypeStruct((D, T), F32),
        scratch_shapes=[pltpu.VMEM((2, te, tm), F32), pltpu.VMEM((2, te, tm), BF16)],
        compiler_params=_params(("arbitrary",)),
        name="peer_dense",
    )(hb, ub, vt, rank2, e2, by_tile(coef), by_tile(cnt))


FINAL_TILE = 512


def _final_kernel(h_ref, yt_ref, g_ref, b_ref, o_ref, *, depth):
    o_ref[...] = _layer_norm(_deepnorm_alpha(depth) * h_ref[...] + yt_ref[...].T, g_ref[...], b_ref[...])


def _final_call(h1, y_t, ln_g, ln_b, depth):
    T, D = h1.shape
    tm = FINAL_TILE
    return pl.pallas_call(
        functools.partial(_final_kernel, depth=depth),
        grid=(T // tm,),
        in_specs=[pl.BlockSpec((tm, D), lambda t: (t, 0)),
                  pl.BlockSpec((D, tm), lambda t: (0, t)),
                  _resident((1, D)), _resident((1, D))],
        out_specs=pl.BlockSpec((tm, D), lambda t: (t, 0)),
        out_shape=jax.ShapeDtypeStruct((T, D), F32),
        compiler_params=_params(("parallel",)),
        name="final_ln",
    )(h1, y_t, ln_g.reshape(1, D), ln_b.reshape(1, D))


def _mixer(h, positions, w_in, conv_w, conv_b, rg_w_a, rg_b_a, rg_w_x, rg_b_x, lru_lambda,
           gn_attn, gn_rec, w_out, ln1_g, ln1_b, depth):
    B, S, D = h.shape
    half = HEAD_DIM // 2
    inv = ROPE_THETA ** (-jnp.arange(half, dtype=F32) / half)
    inv_full = jnp.concatenate([inv, inv]).reshape(1, HEAD_DIM)
    w_qkv = w_in[:, :3 * ATTN_WIDTH].astype(BF16)
    w_rec = w_in[:, 3 * ATTN_WIDTH:].astype(BF16)
    wax = jnp.concatenate([rg_w_a, rg_w_x], axis=-1).astype(BF16)
    q, k, v = _qkv_call(h, positions.reshape(B, S, 1), inv_full, w_qkv)
    rec_n = _rec_call(h, w_rec, conv_w, conv_b, wax, rg_b_a, rg_b_x, lru_lambda, gn_rec)
    attn = _attn_call(q, k, v)
    return _mix_call(h, attn, rec_n, w_out.astype(BF16), gn_attn, ln1_g, ln1_b, depth)


def _peer(h, hb, wq, keys1, keys2, u, v, ln2_g, ln2_b, depth):
    B, S, D = h.shape
    h2 = h.reshape(B * S, D)
    hb2 = hb.reshape(B * S, D)
    wq_h = wq.reshape(D, PEER_HEADS, 2 * PEER_HALF).transpose(1, 0, 2).astype(BF16)
    rank2, e2, coef, cnt = _router_call(hb2, wq_h, keys1.astype(BF16), keys2.astype(BF16))
    ub, vt = _uvprep_call(u, v)
    y_t = _peer_call(hb2, ub, vt, rank2, e2, coef, cnt)
    return _final_call(h2, y_t, ln2_g, ln2_b, depth).reshape(B, S, D)


def kernel(x, positions, w_in, conv_w, conv_b, rg_w_a, rg_b_a, rg_w_x, rg_b_x, lru_lambda, gn_attn, gn_rec,
           w_out, ln1_g, ln1_b, peer_wq, peer_keys1, peer_keys2, peer_u, peer_v, ln2_g, ln2_b):
    depth = w_in.shape[0]
    h = x
    for l in range(depth):
        h, hb = _mixer(h, positions, w_in[l], conv_w[l], conv_b[l], rg_w_a[l], rg_b_a[l], rg_w_x[l], rg_b_x[l],
                       lru_lambda[l], gn_attn[l], gn_rec[l], w_out[l], ln1_g[l], ln1_b[l], depth)
        h = _peer(h, hb, peer_wq[l], peer_keys1[l], peer_keys2[l], peer_u[l], peer_v[l], ln2_g[l], ln2_b[l],
                  depth)
    return h
```

```python
import functools
import math

import jax
import jax.numpy as jnp
import numpy as np
from jax import lax
from jax.experimental import pallas as pl
from jax.experimental.pallas import tpu as pltpu

F32 = jnp.float32
BF16 = jnp.bfloat16

ATTN_HEADS = 8
HEAD_DIM = 128
ATTN_WIDTH = ATTN_HEADS * HEAD_DIM
REC_BLOCKS = 8
REC_BLOCK = 128
REC_WIDTH = REC_BLOCKS * REC_BLOCK
CONV_WIDTH = 4
LRU_C = 8.0
BAND = 128
DILATIONS = (1, 4, 16)
ROPE_THETA = 10000.0
NEG_INF = -1e30
PEER_HEADS = 8
PEER_NKEYS = 128
PEER_EXPERTS = PEER_NKEYS * PEER_NKEYS
PEER_HALF = 128
PEER_TOPK = 16
SUBLANES = 8

V7X_VMEM_LIMIT = 56 * 1024 * 1024


def _params(sem, vmem=V7X_VMEM_LIMIT):
    return pltpu.CompilerParams(dimension_semantics=sem, vmem_limit_bytes=vmem)


def _resident(shape):
    zeros = (0,) * len(shape)
    return pl.BlockSpec(shape, lambda *_: zeros, pipeline_mode=pl.Buffered(1))


def _exact(v):
    assert float(np.float32(v)) == float(v)
    return jnp.full((1, 1), v, F32)


_PI_HI = float(np.float32(math.pi))
_PI_MID = float(np.float32(math.pi - _PI_HI))
_PI_LO = float(np.float32(math.pi - _PI_HI - _PI_MID))


def _ln_eps():
    return _exact(1.0) / _exact(1e5)


def _gelu_tanh(x):
    pi = _exact(_PI_HI) + _exact(_PI_MID) + _exact(_PI_LO)
    c = jnp.sqrt(_exact(2.0) / pi).astype(x.dtype)
    k = (_exact(44715.0) / _exact(1e6)).astype(x.dtype)
    return 0.5 * x * (1.0 + jnp.tanh(c * (x + k * (x * x * x))))


def _sigmoid(x):
    return 1.0 / (1.0 + jnp.exp(-x))


def _rms_scale(x):
    return lax.rsqrt(jnp.mean(x * x, axis=-1, keepdims=True) + _ln_eps())


def _layer_norm(z, g, b):
    mu = jnp.mean(z, axis=-1, keepdims=True)
    zc = z - mu
    var = jnp.mean(zc * zc, axis=-1, keepdims=True)
    return zc * lax.rsqrt(var + _ln_eps()) * g + b


def _deepnorm_alpha(depth):
    return jnp.sqrt(jnp.sqrt(_exact(2.0 * depth)))


QKV_TILE = 512


def _qkv_kernel(x_ref, pos_ref, inv_ref, w_ref, q_ref, k_ref, v_ref):
    proj = jnp.dot(x_ref[0].astype(BF16), w_ref[...], preferred_element_type=F32)
    ang = pos_ref[0].astype(F32) * inv_ref[...]
    lane = lax.broadcasted_iota(jnp.int32, ang.shape, 1)
    cos = jnp.cos(ang)
    sin = jnp.where(lane < HEAD_DIM // 2, -jnp.sin(ang), jnp.sin(ang))
    scale = lax.rsqrt(_exact(float(HEAD_DIM)))
    for h in range(ATTN_HEADS):
        cols = slice(h * HEAD_DIM, (h + 1) * HEAD_DIM)
        qh = proj[:, cols]
        kh = proj[:, ATTN_WIDTH + h * HEAD_DIM: ATTN_WIDTH + (h + 1) * HEAD_DIM]
        q_ref[0, :, cols] = (qh * cos + pltpu.roll(qh, HEAD_DIM // 2, 1) * sin) * scale
        k_ref[0, :, cols] = kh * cos + pltpu.roll(kh, HEAD_DIM // 2, 1) * sin
    v_ref[0] = proj[:, 2 * ATTN_WIDTH:]


def _qkv_call(x, pos, inv_full, w_qkv):
    B, S, D = x.shape
    tm = QKV_TILE
    spec = pl.BlockSpec((1, tm, ATTN_WIDTH), lambda b, s: (b, s, 0))
    shape = jax.ShapeDtypeStruct((B, S, ATTN_WIDTH), F32)
    return pl.pallas_call(
        _qkv_kernel,
        grid=(B, S // tm),
        in_specs=[pl.BlockSpec((1, tm, D), lambda b, s: (b, s, 0)),
                  pl.BlockSpec((1, tm, 1), lambda b, s: (b, s, 0)),
                  _resident((1, HEAD_DIM)),
                  _resident(w_qkv.shape)],
        out_specs=[spec, spec, spec],
        out_shape=[shape, shape, shape],
        compiler_params=_params(("parallel", "parallel")),
        name="qkv_rope",
    )(x, pos, inv_full, w_qkv)


REC_TILE = 512


def _rec_kernel(x_ref, w_ref, cw_ref, cb_ref, wax_ref, ba_ref, bx_ref, lam_ref, g_ref, o_ref,
                ext_ref, a_ref, b_ref, h_ref, hc_ref):
    tm = REC_TILE

    @pl.when(pl.program_id(1) == 0)
    def _():
        ext_ref[0:SUBLANES, :] = jnp.zeros((SUBLANES, REC_WIDTH), F32)
        hc_ref[...] = jnp.zeros((SUBLANES, REC_WIDTH), F32)

    proj = jnp.dot(x_ref[0].astype(BF16), w_ref[...], preferred_element_type=F32)
    xr = proj[:, :REC_WIDTH]
    gr = proj[:, REC_WIDTH:]
    ext_ref[SUBLANES:SUBLANES + tm, :] = xr
    xc = cb_ref[...] + cw_ref[CONV_WIDTH - 1:CONV_WIDTH, :] * xr
    for j in range(CONV_WIDTH - 1):
        back = CONV_WIDTH - 1 - j
        xc = xc + cw_ref[j:j + 1, :] * ext_ref[pl.ds(SUBLANES - back, tm), :]
    ext_ref[0:SUBLANES, :] = xr[tm - SUBLANES:, :]

    rs, is_ = [], []
    for n in range(REC_BLOCKS):
        xb = xc[:, n * REC_BLOCK:(n + 1) * REC_BLOCK].astype(BF16)
        gts = jnp.dot(xb, wax_ref[n], preferred_element_type=F32)
        rs.append(gts[:, :REC_BLOCK])
        is_.append(gts[:, REC_BLOCK:])
    r = _sigmoid(jnp.concatenate(rs, axis=1) + ba_ref[...])
    ig = _sigmoid(jnp.concatenate(is_, axis=1) + bx_ref[...])
    z = -lam_ref[...]
    softplus = jnp.maximum(z, 0.0) + jnp.log(1.0 + jnp.exp(-jnp.abs(z)))
    log_a = (-LRU_C) * r * softplus
    a = jnp.exp(log_a)
    bt = jnp.sqrt(1.0 - jnp.exp(2.0 * log_a)) * (ig * xc)

    r8 = lax.broadcasted_iota(jnp.int32, a.shape, 0) & (SUBLANES - 1)
    for s in (1, 2, 4):
        a_sh = pltpu.roll(a, s, 0)
        b_sh = pltpu.roll(bt, s, 0)
        m = r8 >= s
        bt = jnp.where(m, a * b_sh + bt, bt)
        a = jnp.where(m, a * a_sh, a)
    a_ref[...] = a
    b_ref[...] = bt

    def group(g, hc):
        rows = pl.ds(pl.multiple_of(g * SUBLANES, SUBLANES), SUBLANES)
        h = a_ref[rows, :] * hc + b_ref[rows, :]
        h_ref[rows, :] = h
        return jnp.broadcast_to(h[SUBLANES - 1:SUBLANES, :], (SUBLANES, REC_WIDTH))

    hc_ref[...] = lax.fori_loop(0, tm // SUBLANES, group, hc_ref[...], unroll=8)

    rec = h_ref[...] * _gelu_tanh(gr)
    o_ref[0] = (rec * _rms_scale(rec) * g_ref[...]).astype(BF16)


def _rec_call(x, w_rec, conv_w, conv_b, wax, b_a, b_x, lam, gn_rec):
    B, S, D = x.shape
    tm = REC_TILE
    row = lambda a: a.reshape(1, REC_WIDTH)
    return pl.pallas_call(
        _rec_kernel,
        grid=(B, S // tm),
        in_specs=[pl.BlockSpec((1, tm, D), lambda b, s: (b, s, 0)),
                  _resident(w_rec.shape),
                  _resident((CONV_WIDTH, REC_WIDTH)),
                  _resident((1, REC_WIDTH)),
                  _resident(wax.shape),
                  _resident((1, REC_WIDTH)), _resident((1, REC_WIDTH)),
                  _resident((1, REC_WIDTH)), _resident((1, REC_WIDTH))],
        out_specs=pl.BlockSpec((1, tm, REC_WIDTH), lambda b, s: (b, s, 0)),
        out_shape=jax.ShapeDtypeStruct((B, S, REC_WIDTH), BF16),
        scratch_shapes=[pltpu.VMEM((SUBLANES + tm, REC_WIDTH), F32),
                        pltpu.VMEM((tm, REC_WIDTH), F32),
                        pltpu.VMEM((tm, REC_WIDTH), F32),
                        pltpu.VMEM((tm, REC_WIDTH), F32),
                        pltpu.VMEM((SUBLANES, REC_WIDTH), F32)],
        compiler_params=_params(("parallel", "arbitrary")),
        name="rec_branch",
    )(x, w_rec, conv_w, row(conv_b), wax, row(b_a), row(b_x), row(lam), row(gn_rec))


ATTN_UNROLL = 8


def _band_block(q, k, v, valid):
    s = lax.dot_general(q.astype(BF16), k.astype(BF16), (((1,), (1,)), ((), ())), preferred_element_type=F32)
    s = jnp.where(valid, s, NEG_INF)
    m = jnp.max(s, axis=-1, keepdims=True)
    p = jnp.exp(s - m)
    l = jnp.sum(p, axis=-1, keepdims=True)
    acc = jnp.dot(p.astype(BF16), v.astype(BF16), preferred_element_type=F32)
    return acc, m, l


def _attn_kernel(q_ref, k_ref, v_ref, o_ref, *stats, seq):
    wide = (BAND, HEAD_DIM)

    for p, dil in enumerate(DILATIONS):
        acc_ref, m_ref, l_ref = stats[3 * p: 3 * p + 3]
        n_blk = seq // (dil * BAND)

        def block(idx, carry, dil=dil, n_blk=n_blk, acc_ref=acc_ref, m_ref=m_ref, l_ref=l_ref):
            c = idx // n_blk
            j = idx - c * n_blk
            k0 = jnp.maximum(j - 1, 0) * BAND
            n_keys = 2 * BAND if n_blk > 1 else BAND
            if dil == 1:
                rows = pl.ds(pl.multiple_of(j * BAND, BAND), BAND)
                krows = pl.ds(pl.multiple_of(k0, BAND), n_keys)
            else:
                rows = pl.ds(dil * j * BAND + c, BAND, stride=dil)
                krows = pl.ds(dil * k0 + c, n_keys, stride=dil)
            qi = lax.broadcasted_iota(jnp.int32, (BAND, n_keys), 0)
            kj = lax.broadcasted_iota(jnp.int32, (BAND, n_keys), 1)
            dist = (j * BAND + qi) - (k0 + kj)
            acc, m, l = _band_block(q_ref[0, rows, :], k_ref[0, krows, :], v_ref[0, krows, :],
                                    (dist >= 0) & (dist <= BAND))
            acc_ref[rows, :] = acc
            m_ref[rows, :] = jnp.broadcast_to(m, wide)
            l_ref[rows, :] = jnp.broadcast_to(l, wide)
            return carry

        lax.fori_loop(0, seq // BAND, block, 0, unroll=ATTN_UNROLL)

    def merge(j, carry):
        rows = pl.ds(pl.multiple_of(j * BAND, BAND), BAND)
        ms = [stats[3 * p + 1][rows, :] for p in range(3)]
        mx = jnp.maximum(jnp.maximum(ms[0], ms[1]), ms[2])
        ws = [jnp.exp(mm - mx) for mm in ms]
        num = sum(ws[p] * stats[3 * p][rows, :] for p in range(3))
        den = sum(ws[p] * stats[3 * p + 2][rows, :] for p in range(3))
        o_ref[0, rows, :] = num / den
        return carry

    lax.fori_loop(0, seq // BAND, merge, 0, unroll=ATTN_UNROLL)


def _attn_call(q, k, v):
    B, S, _ = q.shape
    assert S % (max(DILATIONS) * BAND) == 0
    spec = pl.BlockSpec((1, S, HEAD_DIM), lambda b, h: (b, 0, h))
    stat = pltpu.VMEM((S, HEAD_DIM), F32)
    return pl.pallas_call(
        functools.partial(_attn_kernel, seq=S),
        grid=(B, ATTN_HEADS),
        in_specs=[spec] * 3,
        out_specs=spec,
        out_shape=jax.ShapeDtypeStruct((B, S, ATTN_WIDTH), F32),
        scratch_shapes=[stat] * 9,
        compiler_params=_params(("parallel", "parallel")),
        name="dilated_attn",
    )(q, k, v)


MIX_TILE = 512


def _mix_kernel(x_ref, at_ref, rc_ref, wo_ref, ga_ref, g_ref, b_ref, o_ref, ob_ref, *, depth):
    at = at_ref[0]
    at_n = (at * _rms_scale(at) * ga_ref[...]).astype(BF16)
    y = jnp.dot(at_n, wo_ref[:ATTN_WIDTH, :], preferred_element_type=F32)
    y = y + jnp.dot(rc_ref[0], wo_ref[ATTN_WIDTH:, :], preferred_element_type=F32)
    h = _layer_norm(_deepnorm_alpha(depth) * x_ref[0] + y, g_ref[...], b_ref[...])
    o_ref[0] = h
    ob_ref[0] = h.astype(BF16)


def _mix_call(x, attn, rec_n, w_out, gn_attn, ln_g, ln_b, depth):
    B, S, D = x.shape
    tm = MIX_TILE
    row = lambda a, n: a.reshape(1, n)
    tile = lambda n: pl.BlockSpec((1, tm, n), lambda b, s: (b, s, 0))
    return pl.pallas_call(
        functools.partial(_mix_kernel, depth=depth),
        grid=(B, S // tm),
        in_specs=[tile(D), tile(ATTN_WIDTH), tile(REC_WIDTH),
                  _resident(w_out.shape),
                  _resident((1, ATTN_WIDTH)), _resident((1, D)), _resident((1, D))],
        out_specs=[tile(D), tile(D)],
        out_shape=[jax.ShapeDtypeStruct((B, S, D), F32), jax.ShapeDtypeStruct((B, S, D), BF16)],
        compiler_params=_params(("parallel", "parallel")),
        name="mix_out_ln",
    )(x, attn, rec_n, w_out, row(gn_attn, ATTN_WIDTH), row(ln_g, D), row(ln_b, D))


ROUTER_TILE = 512
N_CAND_ROWS = 16 + 7 * 8 + 8
ROUTER_HEADS_PER_TRIP = 1


def _sort_network(n):
    def merge(lo, hi, r):
        step = r * 2
        if step < hi - lo:
            yield from merge(lo, hi, step)
            yield from merge(lo + r, hi, step)
            yield from ((i, i + r) for i in range(lo + r, hi - r, step))
        else:
            yield (lo, lo + r)

    def sort(lo, hi):
        if hi - lo >= 1:
            mid = lo + (hi - lo) // 2
            yield from sort(lo, mid)
            yield from sort(mid + 1, hi)
            yield from merge(lo, hi, 1)

    return tuple(sort(0, n - 1))


_SORT16 = _sort_network(PEER_TOPK)
_BITONIC16 = tuple((i, i + d) for d in (8, 4, 2, 1) for i in range(PEER_TOPK) if i & d == 0)


def _exchange(x, pairs):
    for i, j in pairs:
        a, b = x[i], x[j]
        if a is None:
            x[i], x[j] = b, None
        elif b is not None:
            x[i], x[j] = jnp.maximum(a, b), jnp.minimum(a, b)
    return x


def _top16_sorted(blocks):
    x = _exchange(list(blocks) + [None] * (PEER_TOPK - len(blocks)), _SORT16)
    for shift in (4, 2, 1):
        partner = [None if v is None else pltpu.roll(v, shift, 0) for v in x]
        merged = []
        for i in range(PEER_TOPK):
            a, b = x[i], partner[PEER_TOPK - 1 - i]
            merged.append(b if a is None else a if b is None else jnp.maximum(a, b))
        x = _exchange(merged, _BITONIC16)
    return x


def _sublane_blocks(a):
    return [a[i * SUBLANES:(i + 1) * SUBLANES] for i in range(a.shape[0] // SUBLANES)]


def _count(flags):
    total = sum(jnp.where(f, 1.0, 0.0) for f in flags)
    return jnp.sum(total, axis=0, keepdims=True)


def _route_head(s1, s2):
    b1, b2 = _sublane_blocks(s1), _sublane_blocks(s2)
    v1, v2 = _top16_sorted(b1), _top16_sorted(b2)
    rows = (v1[0].shape[0], v1[0].shape[1])
    sub = lax.broadcasted_iota(jnp.int32, rows, 0)

    def on_sublanes(vals):
        out = vals[0]
        for j in range(1, SUBLANES):
            out = jnp.where(sub == j, vals[j], out)
        return out

    v2_lo, v2_hi, v1_hi = on_sublanes(v2[:8]), on_sublanes(v2[8:]), on_sublanes(v1[8:])
    cand = [v1[0] + v2_lo, v1[0] + v2_hi] + [v1[r] + v2_lo for r in range(1, 8)] + [v1_hi + v2[0]]
    best = _top16_sorted(cand)
    chosen = [c >= best[PEER_TOPK - 1] for c in cand]
    picked = [jnp.where(f, 1.0, 0.0) for f in chosen]
    z = jnp.sum(sum(p * jnp.exp(c - best[0]) for p, c in zip(picked, cand)), axis=0, keepdims=True)
    cnt_r = [jnp.sum(picked[0] + picked[1], axis=0, keepdims=True)]
    cnt_r += [jnp.sum(picked[1 + r], axis=0, keepdims=True) for r in range(1, 8)]
    cnt_r += [picked[9][j:j + 1] for j in range(SUBLANES)]

    cnt_r = [jnp.broadcast_to(c, rows) for c in cnt_r]
    rank2, cnt_k, coef, e2 = [], [], [], []
    inv_z = 1.0 / z
    for blk1, blk2 in zip(b1, b2):
        r2 = jnp.full(rows, float(PEER_TOPK), F32)
        ck = jnp.zeros(rows, F32)
        for r in range(PEER_TOPK - 1, -1, -1):
            r2 = jnp.where(blk2 >= v2[r], float(r), r2)
            ck = jnp.where(blk1 == v1[r], cnt_r[r], ck)
        rank2.append(r2)
        cnt_k.append(ck)
        coef.append(jnp.exp(blk1 - v1[0]) * inv_z)
        e2.append(jnp.exp(blk2 - v2[0]))

    strict = [v[r] > v[r + 1] for v in (v1, v2) for r in range(PEER_TOPK - 1)]
    ok = _count(strict) == float(SUBLANES * len(strict))
    for blocks, vals in ((b1, v1), (b2, v2), (cand, best)):
        ok = ok & (_count([b >= vals[PEER_TOPK - 1] for b in blocks]) == float(PEER_TOPK))
    redo = jnp.max(jnp.where(ok, 0.0, 1.0))
    cat = lambda parts: jnp.concatenate(parts, axis=0)
    return cat(rank2), cat(e2), cat(coef), cat(cnt_k), redo


def _top16_rows_any(s):
    idx = lax.broadcasted_iota(jnp.int32, s.shape, 0)
    rank = jnp.full(s.shape, float(PEER_TOPK), F32)
    vals = []
    for r in range(PEER_TOPK):
        m = jnp.max(s, axis=0, keepdims=True)
        sel = idx == jnp.min(jnp.where(s == m, idx, s.shape[0]), axis=0, keepdims=True)
        rank = jnp.where(sel, float(r), rank)
        s = jnp.where(sel, -jnp.inf, s)
        vals.append(m)
    return rank, jnp.concatenate(vals, axis=0)


def _route_head_any(s1, s2):
    rank1, v1 = _top16_rows_any(s1)
    rank2, v2 = _top16_rows_any(s2)
    blocks = [v1[0:1] + v2]
    for r1 in range(1, 8):
        blocks.append(v1[r1:r1 + 1] + v2[0:8])
    blocks.append(v1[8:16] + v2[0:1])
    cand = jnp.concatenate(blocks, axis=0)
    rank3, _ = _top16_rows_any(cand)
    chosen = jnp.where(rank3 < PEER_TOPK, 1.0, 0.0)
    z = jnp.sum(chosen * jnp.exp(cand - cand[0:1]), axis=0, keepdims=True)
    counts = [jnp.sum(chosen[0:16], axis=0, keepdims=True)]
    for r1 in range(1, 8):
        counts.append(jnp.sum(chosen[8 + 8 * r1: 16 + 8 * r1], axis=0, keepdims=True))
    counts.append(chosen[N_CAND_ROWS - 8:])
    cnt_r = jnp.concatenate(counts, axis=0)
    cnt_k = jnp.zeros(s1.shape, F32)
    for r in range(PEER_TOPK):
        cnt_k = jnp.where(rank1 == r, cnt_r[r:r + 1], cnt_k)
    return rank2, jnp.exp(s2 - v2[0:1]), jnp.exp(s1 - v1[0:1]) / z, cnt_k


def _router_kernel(h_ref, wq_ref, k1_ref, k2_ref, rank2_ref, e2_ref, coef_ref, cnt_ref):
    hb = h_ref[...]

    def emit(h, rank2, e2, coef, cnt_k):
        rank2_ref[h] = rank2.astype(BF16)
        e2_ref[h] = e2.astype(BF16)
        coef_ref[h] = coef
        cnt_ref[h] = cnt_k

    def heads(g, carry):
        todo = []
        for i in range(ROUTER_HEADS_PER_TRIP):
            h = g * ROUTER_HEADS_PER_TRIP + i
            q_t = lax.dot_general(wq_ref[h], hb, (((1,), (1,)), ((), ())), preferred_element_type=F32)
            s1 = jnp.dot(k1_ref[...], q_t[:PEER_HALF].astype(BF16), preferred_element_type=F32)
            s2 = jnp.dot(k2_ref[...], q_t[PEER_HALF:].astype(BF16), preferred_element_type=F32)
            *stats, redo = _route_head(s1, s2)
            emit(h, *stats)
            todo.append((h, s1, s2, redo))
        for h, s1, s2, redo in todo:
            pl.when(redo > 0.0)(functools.partial(lambda h, s1, s2: emit(h, *_route_head_any(s1, s2)), h, s1, s2))
        return carry

    lax.fori_loop(0, PEER_HEADS // ROUTER_HEADS_PER_TRIP, heads, 0)


def _router_call(hb, wq_t, k1, k2):
    T, D = hb.shape
    tm = ROUTER_TILE
    o_spec = pl.BlockSpec((PEER_HEADS, PEER_NKEYS, tm), lambda t: (0, 0, t))
    shape = lambda dt: jax.ShapeDtypeStruct((PEER_HEADS, PEER_NKEYS, T), dt)
    return pl.pallas_call(
        _router_kernel,
        grid=(T // tm,),
        in_specs=[pl.BlockSpec((tm, D), lambda t: (t, 0)),
                  _resident(wq_t.shape), _resident(k1.shape), _resident(k2.shape)],
        out_specs=[o_spec] * 4,
        out_shape=[shape(BF16), shape(BF16), shape(F32), shape(F32)],
        compiler_params=_params(("parallel",)),
        name="peer_router",
    )(hb, wq_t, k1, k2)


PREP_TILE = 512


def _uvprep_kernel(u_ref, v_ref, ub_ref, vt_ref):
    ub_ref[...] = u_ref[...].astype(BF16)
    vt_ref[...] = v_ref[...].T.astype(BF16)


def _uvprep_call(u, v):
    E, D = u.shape
    te = PREP_TILE
    return pl.pallas_call(
        _uvprep_kernel,
        grid=(E // te,),
        in_specs=[pl.BlockSpec((te, D), lambda e: (e, 0)), pl.BlockSpec((te, D), lambda e: (e, 0))],
        out_specs=[pl.BlockSpec((te, D), lambda e: (e, 0)), pl.BlockSpec((D, te), lambda e: (0, e))],
        out_shape=[jax.ShapeDtypeStruct((E, D), BF16), jax.ShapeDtypeStruct((D, E), BF16)],
        compiler_params=_params(("parallel",)),
        name="peer_uvprep",
    )(u, v)


PEER_TOK_TILE = 512
PEER_EXP_TILE = 1024
KEYS_PER_STEP = PEER_EXP_TILE // PEER_NKEYS
PEER_STAGES = 3


def _peer_kernel(h_ref, u_ref, vt_ref, rank2_ref, e2_ref, coef_ref, cnt_ref, o_ref, act_ref, p_ref, *, n_exp_tiles):
    s = pl.program_id(0)
    cur = s % 2
    prev = (s + 1) % 2

    @pl.when(s == 0)
    def _():
        act_ref[1] = jnp.zeros(act_ref.shape[1:], BF16)
        p_ref[0] = jnp.zeros(p_ref.shape[1:], BF16)

    @pl.when(jnp.maximum(s - (PEER_STAGES - 1), 0) % n_exp_tiles == 0)
    def _():
        o_ref[...] = jnp.zeros(o_ref.shape, F32)

    act_ref[cur] = lax.dot_general(u_ref[...], h_ref[...], (((1,), (1,)), ((), ())),
                                   preferred_element_type=F32).astype(BF16)

    for key in range(KEYS_PER_STEP):
        gate = None
        for h in range(PEER_HEADS):
            cnt = cnt_ref[h, key:key + 1, :].astype(BF16)
            cf = coef_ref[h, key:key + 1, :].astype(BF16)
            term = jnp.where(rank2_ref[h] < cnt, e2_ref[h], jnp.zeros((), BF16)) * cf
            gate = term if gate is None else gate + term
        rows = slice(key * PEER_NKEYS, (key + 1) * PEER_NKEYS)
        p_ref[prev, rows, :] = gate * _gelu_tanh(act_ref[prev, rows, :])

    o_ref[...] += jnp.dot(vt_ref[...], p_ref[cur], preferred_element_type=F32)


def _peer_call(hb, ub, vt, rank2, e2, coef, cnt):
    T, D = hb.shape
    tm, te = PEER_TOK_TILE, PEER_EXP_TILE
    n_e = PEER_EXPERTS // te
    n_pairs = (T // tm) * n_e
    by_tile = lambda a: a.reshape(PEER_HEADS, n_e, KEYS_PER_STEP, T)

    def pair(s, lag):
        i = jnp.clip(s - lag, 0, n_pairs - 1)
        return i // n_e, i % n_e

    head_spec = pl.BlockSpec((PEER_HEADS, PEER_NKEYS, tm), lambda s: (0, 0, pair(s, 1)[0]))
    key_spec = pl.BlockSpec((PEER_HEADS, None, KEYS_PER_STEP, tm),
                            lambda s: (0, pair(s, 1)[1], 0, pair(s, 1)[0]))
    return pl.pallas_call(
        functools.partial(_peer_kernel, n_exp_tiles=n_e),
        grid=(n_pairs + PEER_STAGES - 1,),
        in_specs=[pl.BlockSpec((tm, D), lambda s: (pair(s, 0)[0], 0)),
                  pl.BlockSpec((te, D), lambda s: (pair(s, 0)[1], 0)),
                  pl.BlockSpec((D, te), lambda s: (0, pair(s, 2)[1])),
                  head_spec, head_spec, key_spec, key_spec],
        out_specs=pl.BlockSpec((D, tm), lambda s: (0, pair(s, 2)[0])),
        out_shape=jax.ShapeDtypeStruct((D, T), F32),
        scratch_shapes=[pltpu.VMEM((2, te, tm), BF16), pltpu.VMEM((2, te, tm), BF16)],
        compiler_params=_params(("arbitrary",)),
        name="peer_dense",
    )(hb, ub, vt, rank2, e2, by_tile(coef), by_tile(cnt))


FINAL_TILE = 512


def _final_kernel(h_ref, yt_ref, g_ref, b_ref, o_ref, *, depth):
    o_ref[...] = _layer_norm(_deepnorm_alpha(depth) * h_ref[...] + yt_ref[...].T, g_ref[...], b_ref[...])


def _final_call(h1, y_t, ln_g, ln_b, depth):
    T, D = h1.shape
    tm = FINAL_TILE
    return pl.pallas_call(
        functools.partial(_final_kernel, depth=depth),
        grid=(T // tm,),
        in_specs=[pl.BlockSpec((tm, D), lambda t: (t, 0)),
                  pl.BlockSpec((D, tm), lambda t: (0, t)),
                  _resident((1, D)), _resident((1, D))],
        out_specs=pl.BlockSpec((tm, D), lambda t: (t, 0)),
        out_shape=jax.ShapeDtypeStruct((T, D), F32),
        compiler_params=_params(("parallel",)),
        name="final_ln",
    )(h1, y_t, ln_g.reshape(1, D), ln_b.reshape(1, D))


def _mixer(h, positions, w_in, conv_w, conv_b, rg_w_a, rg_b_a, rg_w_x, rg_b_x, lru_lambda,
           gn_attn, gn_rec, w_out, ln1_g, ln1_b, depth):
    B, S, D = h.shape
    half = HEAD_DIM // 2
    inv = ROPE_THETA ** (-jnp.arange(half, dtype=F32) / half)
    inv_full = jnp.concatenate([inv, inv]).reshape(1, HEAD_DIM)
    w_qkv = w_in[:, :3 * ATTN_WIDTH].astype(BF16)
    w_rec = w_in[:, 3 * ATTN_WIDTH:].astype(BF16)
    wax = jnp.concatenate([rg_w_a, rg_w_x], axis=-1).astype(BF16)
    q, k, v = _qkv_call(h, positions.reshape(B, S, 1), inv_full, w_qkv)
    rec_n = _rec_call(h, w_rec, conv_w, conv_b, wax, rg_b_a, rg_b_x, lru_lambda, gn_rec)
    attn = _attn_call(q, k, v)
    return _mix_call(h, attn, rec_n, w_out.astype(BF16), gn_attn, ln1_g, ln1_b, depth)


def _peer(h, hb, wq, keys1, keys2, u, v, ln2_g, ln2_b, depth):
    B, S, D = h.shape
    h2 = h.reshape(B * S, D)
    hb2 = hb.reshape(B * S, D)
    wq_t = wq.reshape(D, PEER_HEADS, 2 * PEER_HALF).transpose(1, 2, 0).astype(BF16)
    rank2, e2, coef, cnt = _router_call(hb2, wq_t, keys1.astype(BF16), keys2.astype(BF16))
    ub, vt = _uvprep_call(u, v)
    y_t = _peer_call(hb2, ub, vt, rank2, e2, coef, cnt)
    return _final_call(h2, y_t, ln2_g, ln2_b, depth).reshape(B, S, D)


def kernel(x, positions, w_in, conv_w, conv_b, rg_w_a, rg_b_a, rg_w_x, rg_b_x, lru_lambda, gn_attn, gn_rec,
           w_out, ln1_g, ln1_b, peer_wq, peer_keys1, peer_keys2, peer_u, peer_v, ln2_g, ln2_b):
    depth = w_in.shape[0]
    h = x
    for l in range(depth):
        h, hb = _mixer(h, positions, w_in[l], conv_w[l], conv_b[l], rg_w_a[l], rg_b_a[l], rg_w_x[l], rg_b_x[l],
                       lru_lambda[l], gn_attn[l], gn_rec[l], w_out[l], ln1_g[l], ln1_b[l], depth)
        h = _peer(h, hb, peer_wq[l], peer_keys1[l], peer_keys2[l], peer_u[l], peer_v[l], ln2_g[l], ln2_b[l],
                  depth)
    return h
```

```python
import functools
import math

import jax
import jax.numpy as jnp
import numpy as np
from jax import lax
from jax.experimental import pallas as pl
from jax.experimental.pallas import tpu as pltpu

F32 = jnp.float32
BF16 = jnp.bfloat16

ATTN_HEADS = 8
HEAD_DIM = 128
ATTN_WIDTH = ATTN_HEADS * HEAD_DIM
REC_BLOCKS = 8
REC_BLOCK = 128
REC_WIDTH = REC_BLOCKS * REC_BLOCK
CONV_WIDTH = 4
LRU_C = 8.0
BAND = 128
DILATIONS = (1, 4, 16)
ROPE_THETA = 10000.0
NEG_INF = -1e30
PEER_HEADS = 8
PEER_NKEYS = 128
PEER_EXPERTS = PEER_NKEYS * PEER_NKEYS
PEER_HALF = 128
PEER_TOPK = 16
SUBLANES = 8

V7X_VMEM_LIMIT = 56 * 1024 * 1024


def _params(sem, vmem=V7X_VMEM_LIMIT):
    return pltpu.CompilerParams(dimension_semantics=sem, vmem_limit_bytes=vmem)


def _resident(shape):
    zeros = (0,) * len(shape)
    return pl.BlockSpec(shape, lambda *_: zeros, pipeline_mode=pl.Buffered(1))


def _exact(v):
    assert float(np.float32(v)) == float(v)
    return jnp.full((1, 1), v, F32)


_PI_HI = float(np.float32(math.pi))
_PI_MID = float(np.float32(math.pi - _PI_HI))
_PI_LO = float(np.float32(math.pi - _PI_HI - _PI_MID))


def _ln_eps():
    return _exact(1.0) / _exact(1e5)


def _gelu_tanh(x):
    pi = _exact(_PI_HI) + _exact(_PI_MID) + _exact(_PI_LO)
    c = jnp.sqrt(_exact(2.0) / pi).astype(x.dtype)
    k = (_exact(44715.0) / _exact(1e6)).astype(x.dtype)
    return 0.5 * x * (1.0 + jnp.tanh(c * (x + k * (x * x * x))))


def _sigmoid(x):
    return 1.0 / (1.0 + jnp.exp(-x))


def _rms_scale(x):
    return lax.rsqrt(jnp.mean(x * x, axis=-1, keepdims=True) + _ln_eps())


def _layer_norm(z, g, b):
    mu = jnp.mean(z, axis=-1, keepdims=True)
    zc = z - mu
    var = jnp.mean(zc * zc, axis=-1, keepdims=True)
    return zc * lax.rsqrt(var + _ln_eps()) * g + b


def _deepnorm_alpha(depth):
    return jnp.sqrt(jnp.sqrt(_exact(2.0 * depth)))


QKV_TILE = 512


def _qkv_kernel(x_ref, pos_ref, inv_ref, w_ref, q_ref, k_ref, v_ref):
    proj = jnp.dot(x_ref[0].astype(BF16), w_ref[...], preferred_element_type=F32)
    ang = pos_ref[0].astype(F32) * inv_ref[...]
    lane = lax.broadcasted_iota(jnp.int32, ang.shape, 1)
    cos = jnp.cos(ang)
    sin = jnp.where(lane < HEAD_DIM // 2, -jnp.sin(ang), jnp.sin(ang))
    scale = lax.rsqrt(_exact(float(HEAD_DIM)))
    for h in range(ATTN_HEADS):
        cols = slice(h * HEAD_DIM, (h + 1) * HEAD_DIM)
        qh = proj[:, cols]
        kh = proj[:, ATTN_WIDTH + h * HEAD_DIM: ATTN_WIDTH + (h + 1) * HEAD_DIM]
        q_ref[0, :, cols] = (qh * cos + pltpu.roll(qh, HEAD_DIM // 2, 1) * sin) * scale
        k_ref[0, :, cols] = kh * cos + pltpu.roll(kh, HEAD_DIM // 2, 1) * sin
    v_ref[0] = proj[:, 2 * ATTN_WIDTH:]


def _qkv_call(x, pos, inv_full, w_qkv):
    B, S, D = x.shape
    tm = QKV_TILE
    spec = pl.BlockSpec((1, tm, ATTN_WIDTH), lambda b, s: (b, s, 0))
    shape = jax.ShapeDtypeStruct((B, S, ATTN_WIDTH), F32)
    return pl.pallas_call(
        _qkv_kernel,
        grid=(B, S // tm),
        in_specs=[pl.BlockSpec((1, tm, D), lambda b, s: (b, s, 0)),
                  pl.BlockSpec((1, tm, 1), lambda b, s: (b, s, 0)),
                  _resident((1, HEAD_DIM)),
                  _resident(w_qkv.shape)],
        out_specs=[spec, spec, spec],
        out_shape=[shape, shape, shape],
        compiler_params=_params(("parallel", "parallel")),
        name="qkv_rope",
    )(x, pos, inv_full, w_qkv)


REC_TILE = 512


def _rec_kernel(x_ref, w_ref, cw_ref, cb_ref, wax_ref, ba_ref, bx_ref, lam_ref, g_ref, o_ref,
                ext_ref, a_ref, b_ref, h_ref, hc_ref):
    tm = REC_TILE

    @pl.when(pl.program_id(1) == 0)
    def _():
        ext_ref[0:SUBLANES, :] = jnp.zeros((SUBLANES, REC_WIDTH), F32)
        hc_ref[...] = jnp.zeros((SUBLANES, REC_WIDTH), F32)

    proj = jnp.dot(x_ref[0].astype(BF16), w_ref[...], preferred_element_type=F32)
    xr = proj[:, :REC_WIDTH]
    gr = proj[:, REC_WIDTH:]
    ext_ref[SUBLANES:SUBLANES + tm, :] = xr
    xc = cb_ref[...] + cw_ref[CONV_WIDTH - 1:CONV_WIDTH, :] * xr
    for j in range(CONV_WIDTH - 1):
        back = CONV_WIDTH - 1 - j
        xc = xc + cw_ref[j:j + 1, :] * ext_ref[pl.ds(SUBLANES - back, tm), :]
    ext_ref[0:SUBLANES, :] = xr[tm - SUBLANES:, :]

    rs, is_ = [], []
    for n in range(REC_BLOCKS):
        xb = xc[:, n * REC_BLOCK:(n + 1) * REC_BLOCK].astype(BF16)
        gts = jnp.dot(xb, wax_ref[n], preferred_element_type=F32)
        rs.append(gts[:, :REC_BLOCK])
        is_.append(gts[:, REC_BLOCK:])
    r = _sigmoid(jnp.concatenate(rs, axis=1) + ba_ref[...])
    ig = _sigmoid(jnp.concatenate(is_, axis=1) + bx_ref[...])
    z = -lam_ref[...]
    softplus = jnp.maximum(z, 0.0) + jnp.log(1.0 + jnp.exp(-jnp.abs(z)))
    log_a = (-LRU_C) * r * softplus
    a = jnp.exp(log_a)
    bt = jnp.sqrt(1.0 - jnp.exp(2.0 * log_a)) * (ig * xc)

    r8 = lax.broadcasted_iota(jnp.int32, a.shape, 0) & (SUBLANES - 1)
    for s in (1, 2, 4):
        a_sh = pltpu.roll(a, s, 0)
        b_sh = pltpu.roll(bt, s, 0)
        m = r8 >= s
        bt = jnp.where(m, a * b_sh + bt, bt)
        a = jnp.where(m, a * a_sh, a)
    a_ref[...] = a
    b_ref[...] = bt

    def group(g, hc):
        rows = pl.ds(pl.multiple_of(g * SUBLANES, SUBLANES), SUBLANES)
        h = a_ref[rows, :] * hc + b_ref[rows, :]
        h_ref[rows, :] = h
        return jnp.broadcast_to(h[SUBLANES - 1:SUBLANES, :], (SUBLANES, REC_WIDTH))

    hc_ref[...] = lax.fori_loop(0, tm // SUBLANES, group, hc_ref[...], unroll=8)

    rec = h_ref[...] * _gelu_tanh(gr)
    o_ref[0] = (rec * _rms_scale(rec) * g_ref[...]).astype(BF16)


def _rec_call(x, w_rec, conv_w, conv_b, wax, b_a, b_x, lam, gn_rec):
    B, S, D = x.shape
    tm = REC_TILE
    row = lambda a: a.reshape(1, REC_WIDTH)
    return pl.pallas_call(
        _rec_kernel,
        grid=(B, S // tm),
        in_specs=[pl.BlockSpec((1, tm, D), lambda b, s: (b, s, 0)),
                  _resident(w_rec.shape),
                  _resident((CONV_WIDTH, REC_WIDTH)),
                  _resident((1, REC_WIDTH)),
                  _resident(wax.shape),
                  _resident((1, REC_WIDTH)), _resident((1, REC_WIDTH)),
                  _resident((1, REC_WIDTH)), _resident((1, REC_WIDTH))],
        out_specs=pl.BlockSpec((1, tm, REC_WIDTH), lambda b, s: (b, s, 0)),
        out_shape=jax.ShapeDtypeStruct((B, S, REC_WIDTH), BF16),
        scratch_shapes=[pltpu.VMEM((SUBLANES + tm, REC_WIDTH), F32),
                        pltpu.VMEM((tm, REC_WIDTH), F32),
                        pltpu.VMEM((tm, REC_WIDTH), F32),
                        pltpu.VMEM((tm, REC_WIDTH), F32),
                        pltpu.VMEM((SUBLANES, REC_WIDTH), F32)],
        compiler_params=_params(("parallel", "arbitrary")),
        name="rec_branch",
    )(x, w_rec, conv_w, row(conv_b), wax, row(b_a), row(b_x), row(lam), row(gn_rec))


ATTN_UNROLL = 16


def _band_block(q, k, v, valid):
    s = lax.dot_general(q.astype(BF16), k.astype(BF16), (((1,), (1,)), ((), ())), preferred_element_type=F32)
    s = jnp.where(valid, s, NEG_INF)
    m = jnp.max(s, axis=-1, keepdims=True)
    p = jnp.exp(s - m)
    l = jnp.sum(p, axis=-1, keepdims=True)
    acc = jnp.dot(p.astype(BF16), v.astype(BF16), preferred_element_type=F32)
    return acc, m, l


def _attn_kernel(q_ref, k_ref, v_ref, o_ref, *stats, seq):
    wide = (BAND, HEAD_DIM)

    for p, dil in enumerate(DILATIONS):
        acc_ref, m_ref, l_ref = stats[3 * p: 3 * p + 3]
        n_blk = seq // (dil * BAND)

        def block(idx, carry, dil=dil, n_blk=n_blk, acc_ref=acc_ref, m_ref=m_ref, l_ref=l_ref):
            c = idx // n_blk
            j = idx - c * n_blk
            k0 = jnp.maximum(j - 1, 0) * BAND
            n_keys = 2 * BAND if n_blk > 1 else BAND
            if dil == 1:
                rows = pl.ds(pl.multiple_of(j * BAND, BAND), BAND)
                krows = pl.ds(pl.multiple_of(k0, BAND), n_keys)
            else:
                rows = pl.ds(dil * j * BAND + c, BAND, stride=dil)
                krows = pl.ds(dil * k0 + c, n_keys, stride=dil)
            qi = lax.broadcasted_iota(jnp.int32, (BAND, n_keys), 0)
            kj = lax.broadcasted_iota(jnp.int32, (BAND, n_keys), 1)
            dist = (j * BAND + qi) - (k0 + kj)
            acc, m, l = _band_block(q_ref[0, rows, :], k_ref[0, krows, :], v_ref[0, krows, :],
                                    (dist >= 0) & (dist <= BAND))
            acc_ref[rows, :] = acc
            m_ref[rows, :] = jnp.broadcast_to(m, wide)
            l_ref[rows, :] = jnp.broadcast_to(l, wide)
            return carry

        lax.fori_loop(0, seq // BAND, block, 0, unroll=ATTN_UNROLL)

    def merge(j, carry):
        rows = pl.ds(pl.multiple_of(j * BAND, BAND), BAND)
        ms = [stats[3 * p + 1][rows, :] for p in range(3)]
        mx = jnp.maximum(jnp.maximum(ms[0], ms[1]), ms[2])
        ws = [jnp.exp(mm - mx) for mm in ms]
        num = sum(ws[p] * stats[3 * p][rows, :] for p in range(3))
        den = sum(ws[p] * stats[3 * p + 2][rows, :] for p in range(3))
        o_ref[0, rows, :] = num / den
        return carry

    lax.fori_loop(0, seq // BAND, merge, 0, unroll=ATTN_UNROLL)


def _attn_call(q, k, v):
    B, S, _ = q.shape
    assert S % (max(DILATIONS) * BAND) == 0
    spec = pl.BlockSpec((1, S, HEAD_DIM), lambda b, h: (b, 0, h))
    stat = pltpu.VMEM((S, HEAD_DIM), F32)
    return pl.pallas_call(
        functools.partial(_attn_kernel, seq=S),
        grid=(B, ATTN_HEADS),
        in_specs=[spec] * 3,
        out_specs=spec,
        out_shape=jax.ShapeDtypeStruct((B, S, ATTN_WIDTH), F32),
        scratch_shapes=[stat] * 9,
        compiler_params=_params(("parallel", "parallel")),
        name="dilated_attn",
    )(q, k, v)


MIX_TILE = 512


def _mix_kernel(x_ref, at_ref, rc_ref, wo_ref, ga_ref, g_ref, b_ref, o_ref, ob_ref, *, depth):
    at = at_ref[0]
    at_n = (at * _rms_scale(at) * ga_ref[...]).astype(BF16)
    y = jnp.dot(at_n, wo_ref[:ATTN_WIDTH, :], preferred_element_type=F32)
    y = y + jnp.dot(rc_ref[0], wo_ref[ATTN_WIDTH:, :], preferred_element_type=F32)
    h = _layer_norm(_deepnorm_alpha(depth) * x_ref[0] + y, g_ref[...], b_ref[...])
    o_ref[0] = h
    ob_ref[0] = h.astype(BF16)


def _mix_call(x, attn, rec_n, w_out, gn_attn, ln_g, ln_b, depth):
    B, S, D = x.shape
    tm = MIX_TILE
    row = lambda a, n: a.reshape(1, n)
    tile = lambda n: pl.BlockSpec((1, tm, n), lambda b, s: (b, s, 0))
    return pl.pallas_call(
        functools.partial(_mix_kernel, depth=depth),
        grid=(B, S // tm),
        in_specs=[tile(D), tile(ATTN_WIDTH), tile(REC_WIDTH),
                  _resident(w_out.shape),
                  _resident((1, ATTN_WIDTH)), _resident((1, D)), _resident((1, D))],
        out_specs=[tile(D), tile(D)],
        out_shape=[jax.ShapeDtypeStruct((B, S, D), F32), jax.ShapeDtypeStruct((B, S, D), BF16)],
        compiler_params=_params(("parallel", "parallel")),
        name="mix_out_ln",
    )(x, attn, rec_n, w_out, row(gn_attn, ATTN_WIDTH), row(ln_g, D), row(ln_b, D))


ROUTER_TILE = 1024
N_CAND_ROWS = 16 + 7 * 8 + 8
ROUTER_HEADS_PER_TRIP = 1


def _sort_network(n):
    def merge(lo, hi, r):
        step = r * 2
        if step < hi - lo:
            yield from merge(lo, hi, step)
            yield from merge(lo + r, hi, step)
            yield from ((i, i + r) for i in range(lo + r, hi - r, step))
        else:
            yield (lo, lo + r)

    def sort(lo, hi):
        if hi - lo >= 1:
            mid = lo + (hi - lo) // 2
            yield from sort(lo, mid)
            yield from sort(mid + 1, hi)
            yield from merge(lo, hi, 1)

    return tuple(sort(0, n - 1))


_SORT16 = _sort_network(PEER_TOPK)
_BITONIC16 = tuple((i, i + d) for d in (8, 4, 2, 1) for i in range(PEER_TOPK) if i & d == 0)


def _exchange(x, pairs):
    for i, j in pairs:
        a, b = x[i], x[j]
        if a is None:
            x[i], x[j] = b, None
        elif b is not None:
            x[i], x[j] = jnp.maximum(a, b), jnp.minimum(a, b)
    return x


def _top16_sorted(blocks):
    x = _exchange(list(blocks) + [None] * (PEER_TOPK - len(blocks)), _SORT16)
    for shift in (4, 2, 1):
        partner = [None if v is None else pltpu.roll(v, shift, 0) for v in x]
        merged = []
        for i in range(PEER_TOPK):
            a, b = x[i], partner[PEER_TOPK - 1 - i]
            merged.append(b if a is None else a if b is None else jnp.maximum(a, b))
        x = _exchange(merged, _BITONIC16)
    return x


def _sublane_blocks(a):
    return [a[i * SUBLANES:(i + 1) * SUBLANES] for i in range(a.shape[0] // SUBLANES)]


def _count(flags):
    total = sum(jnp.where(f, 1.0, 0.0) for f in flags)
    return jnp.sum(total, axis=0, keepdims=True)


def _route_head(s1, s2):
    b1, b2 = _sublane_blocks(s1), _sublane_blocks(s2)
    v1, v2 = _top16_sorted(b1), _top16_sorted(b2)
    rows = (v1[0].shape[0], v1[0].shape[1])
    sub = lax.broadcasted_iota(jnp.int32, rows, 0)

    def on_sublanes(vals):
        out = vals[0]
        for j in range(1, SUBLANES):
            out = jnp.where(sub == j, vals[j], out)
        return out

    v2_lo, v2_hi, v1_hi = on_sublanes(v2[:8]), on_sublanes(v2[8:]), on_sublanes(v1[8:])
    cand = [v1[0] + v2_lo, v1[0] + v2_hi] + [v1[r] + v2_lo for r in range(1, 8)] + [v1_hi + v2[0]]
    best = _top16_sorted(cand)
    chosen = [c >= best[PEER_TOPK - 1] for c in cand]
    picked = [jnp.where(f, 1.0, 0.0) for f in chosen]
    z = jnp.sum(sum(p * jnp.exp(c - best[0]) for p, c in zip(picked, cand)), axis=0, keepdims=True)
    cnt_r = [jnp.sum(picked[0] + picked[1], axis=0, keepdims=True)]
    cnt_r += [jnp.sum(picked[1 + r], axis=0, keepdims=True) for r in range(1, 8)]
    cnt_r += [picked[9][j:j + 1] for j in range(SUBLANES)]

    cnt_r = [jnp.broadcast_to(c, rows) for c in cnt_r]
    rank2, cnt_k, coef, e2 = [], [], [], []
    inv_z = 1.0 / z
    for blk1, blk2 in zip(b1, b2):
        r2 = jnp.full(rows, float(PEER_TOPK), F32)
        ck = jnp.zeros(rows, F32)
        for r in range(PEER_TOPK - 1, -1, -1):
            r2 = jnp.where(blk2 >= v2[r], float(r), r2)
            ck = jnp.where(blk1 == v1[r], cnt_r[r], ck)
        rank2.append(r2)
        cnt_k.append(ck)
        coef.append(jnp.exp(blk1 - v1[0]) * inv_z)
        e2.append(jnp.exp(blk2 - v2[0]))

    strict = [v[r] > v[r + 1] for v in (v1, v2) for r in range(PEER_TOPK - 1)]
    ok = _count(strict) == float(SUBLANES * len(strict))
    for blocks, vals in ((b1, v1), (b2, v2), (cand, best)):
        ok = ok & (_count([b >= vals[PEER_TOPK - 1] for b in blocks]) == float(PEER_TOPK))
    redo = jnp.max(jnp.where(ok, 0.0, 1.0))
    cat = lambda parts: jnp.concatenate(parts, axis=0)
    return cat(rank2), cat(e2), cat(coef), cat(cnt_k), redo


def _top16_rows_any(s):
    idx = lax.broadcasted_iota(jnp.int32, s.shape, 0)
    rank = jnp.full(s.shape, float(PEER_TOPK), F32)
    vals = []
    for r in range(PEER_TOPK):
        m = jnp.max(s, axis=0, keepdims=True)
        sel = idx == jnp.min(jnp.where(s == m, idx, s.shape[0]), axis=0, keepdims=True)
        rank = jnp.where(sel, float(r), rank)
        s = jnp.where(sel, -jnp.inf, s)
        vals.append(m)
    return rank, jnp.concatenate(vals, axis=0)


def _route_head_any(s1, s2):
    rank1, v1 = _top16_rows_any(s1)
    rank2, v2 = _top16_rows_any(s2)
    blocks = [v1[0:1] + v2]
    for r1 in range(1, 8):
        blocks.append(v1[r1:r1 + 1] + v2[0:8])
    blocks.append(v1[8:16] + v2[0:1])
    cand = jnp.concatenate(blocks, axis=0)
    rank3, _ = _top16_rows_any(cand)
    chosen = jnp.where(rank3 < PEER_TOPK, 1.0, 0.0)
    z = jnp.sum(chosen * jnp.exp(cand - cand[0:1]), axis=0, keepdims=True)
    counts = [jnp.sum(chosen[0:16], axis=0, keepdims=True)]
    for r1 in range(1, 8):
        counts.append(jnp.sum(chosen[8 + 8 * r1: 16 + 8 * r1], axis=0, keepdims=True))
    counts.append(chosen[N_CAND_ROWS - 8:])
    cnt_r = jnp.concatenate(counts, axis=0)
    cnt_k = jnp.zeros(s1.shape, F32)
    for r in range(PEER_TOPK):
        cnt_k = jnp.where(rank1 == r, cnt_r[r:r + 1], cnt_k)
    return rank2, jnp.exp(s2 - v2[0:1]), jnp.exp(s1 - v1[0:1]) / z, cnt_k


def _router_kernel(h_ref, wq_ref, k1_ref, k2_ref, rank2_ref, e2_ref, coef_ref, cnt_ref):
    hb = h_ref[...]

    def emit(h, rank2, e2, coef, cnt_k):
        rank2_ref[h] = rank2.astype(BF16)
        e2_ref[h] = e2.astype(BF16)
        coef_ref[h] = coef
        cnt_ref[h] = cnt_k

    def heads(g, carry):
        todo = []
        for i in range(ROUTER_HEADS_PER_TRIP):
            h = g * ROUTER_HEADS_PER_TRIP + i
            q_t = lax.dot_general(wq_ref[h], hb, (((1,), (1,)), ((), ())), preferred_element_type=F32)
            s1 = jnp.dot(k1_ref[...], q_t[:PEER_HALF].astype(BF16), preferred_element_type=F32)
            s2 = jnp.dot(k2_ref[...], q_t[PEER_HALF:].astype(BF16), preferred_element_type=F32)
            *stats, redo = _route_head(s1, s2)
            emit(h, *stats)
            todo.append((h, s1, s2, redo))
        for h, s1, s2, redo in todo:
            pl.when(redo > 0.0)(functools.partial(lambda h, s1, s2: emit(h, *_route_head_any(s1, s2)), h, s1, s2))
        return carry

    lax.fori_loop(0, PEER_HEADS // ROUTER_HEADS_PER_TRIP, heads, 0)


def _router_call(hb, wq_t, k1, k2):
    T, D = hb.shape
    tm = ROUTER_TILE
    o_spec = pl.BlockSpec((PEER_HEADS, PEER_NKEYS, tm), lambda t: (0, 0, t))
    shape = lambda dt: jax.ShapeDtypeStruct((PEER_HEADS, PEER_NKEYS, T), dt)
    return pl.pallas_call(
        _router_kernel,
        grid=(T // tm,),
        in_specs=[pl.BlockSpec((tm, D), lambda t: (t, 0)),
                  _resident(wq_t.shape), _resident(k1.shape), _resident(k2.shape)],
        out_specs=[o_spec] * 4,
        out_shape=[shape(BF16), shape(BF16), shape(F32), shape(F32)],
        compiler_params=_params(("parallel",)),
        name="peer_router",
    )(hb, wq_t, k1, k2)


PREP_TILE = 512


def _uvprep_kernel(u_ref, v_ref, ub_ref, vt_ref):
    ub_ref[...] = u_ref[...].astype(BF16)
    vt_ref[...] = v_ref[...].T.astype(BF16)


def _uvprep_call(u, v):
    E, D = u.shape
    te = PREP_TILE
    return pl.pallas_call(
        _uvprep_kernel,
        grid=(E // te,),
        in_specs=[pl.BlockSpec((te, D), lambda e: (e, 0)), pl.BlockSpec((te, D), lambda e: (e, 0))],
        out_specs=[pl.BlockSpec((te, D), lambda e: (e, 0)), pl.BlockSpec((D, te), lambda e: (0, e))],
        out_shape=[jax.ShapeDtypeStruct((E, D), BF16), jax.ShapeDtypeStruct((D, E), BF16)],
        compiler_params=_params(("parallel",)),
        name="peer_uvprep",
    )(u, v)


PEER_TOK_TILE = 512
PEER_EXP_TILE = 1024
KEYS_PER_STEP = PEER_EXP_TILE // PEER_NKEYS
PEER_STAGES = 3


def _peer_kernel(h_ref, u_ref, vt_ref, rank2_ref, e2_ref, coef_ref, cnt_ref, o_ref, act_ref, p_ref, *, n_exp_tiles):
    s = pl.program_id(0)
    cur = s % 2
    prev = (s + 1) % 2

    @pl.when(s == 0)
    def _():
        act_ref[1] = jnp.zeros(act_ref.shape[1:], BF16)
        p_ref[0] = jnp.zeros(p_ref.shape[1:], BF16)

    @pl.when(jnp.maximum(s - (PEER_STAGES - 1), 0) % n_exp_tiles == 0)
    def _():
        o_ref[...] = jnp.zeros(o_ref.shape, F32)

    act_ref[cur] = lax.dot_general(u_ref[...], h_ref[...], (((1,), (1,)), ((), ())),
                                   preferred_element_type=F32).astype(BF16)

    for key in range(KEYS_PER_STEP):
        gate = None
        for h in range(PEER_HEADS):
            cnt = cnt_ref[h, key:key + 1, :].astype(BF16)
            cf = coef_ref[h, key:key + 1, :].astype(BF16)
            term = jnp.where(rank2_ref[h] < cnt, e2_ref[h], jnp.zeros((), BF16)) * cf
            gate = term if gate is None else gate + term
        rows = slice(key * PEER_NKEYS, (key + 1) * PEER_NKEYS)
        p_ref[prev, rows, :] = gate * _gelu_tanh(act_ref[prev, rows, :])

    o_ref[...] += jnp.dot(vt_ref[...], p_ref[cur], preferred_element_type=F32)


def _peer_call(hb, ub, vt, rank2, e2, coef, cnt):
    T, D = hb.shape
    tm, te = PEER_TOK_TILE, PEER_EXP_TILE
    n_e = PEER_EXPERTS // te
    n_pairs = (T // tm) * n_e
    by_tile = lambda a: a.reshape(PEER_HEADS, n_e, KEYS_PER_STEP, T)

    def pair(s, lag):
        i = jnp.clip(s - lag, 0, n_pairs - 1)
        return i // n_e, i % n_e

    head_spec = pl.BlockSpec((PEER_HEADS, PEER_NKEYS, tm), lambda s: (0, 0, pair(s, 1)[0]))
    key_spec = pl.BlockSpec((PEER_HEADS, None, KEYS_PER_STEP, tm),
                            lambda s: (0, pair(s, 1)[1], 0, pair(s, 1)[0]))
    return pl.pallas_call(
        functools.partial(_peer_kernel, n_exp_tiles=n_e),
        grid=(n_pairs + PEER_STAGES - 1,),
        in_specs=[pl.BlockSpec((tm, D), lambda s: (pair(s, 0)[0], 0)),
                  pl.BlockSpec((te, D), lambda s: (pair(s, 0)[1], 0)),
                  pl.BlockSpec((D, te), lambda s: (0, pair(s, 2)[1])),
                  head_spec, head_spec, key_spec, key_spec],
        out_specs=pl.BlockSpec((D, tm), lambda s: (0, pair(s, 2)[0])),
        out_shape=jax.ShapeDtypeStruct((D, T), F32),
        scratch_shapes=[pltpu.VMEM((2, te, tm), BF16), pltpu.VMEM((2, te, tm), BF16)],
        compiler_params=_params(("arbitrary",)),
        name="peer_dense",
    )(hb, ub, vt, rank2, e2, by_tile(coef), by_tile(cnt))


FINAL_TILE = 512


def _final_kernel(h_ref, yt_ref, g_ref, b_ref, o_ref, *, depth):
    o_ref[...] = _layer_norm(_deepnorm_alpha(depth) * h_ref[...] + yt_ref[...].T, g_ref[...], b_ref[...])


def _final_call(h1, y_t, ln_g, ln_b, depth):
    T, D = h1.shape
    tm = FINAL_TILE
    return pl.pallas_call(
        functools.partial(_final_kernel, depth=depth),
        grid=(T // tm,),
        in_specs=[pl.BlockSpec((tm, D), lambda t: (t, 0)),
                  pl.BlockSpec((D, tm), lambda t: (0, t)),
                  _resident((1, D)), _resident((1, D))],
        out_specs=pl.BlockSpec((tm, D), lambda t: (t, 0)),
        out_shape=jax.ShapeDtypeStruct((T, D), F32),
        compiler_params=_params(("parallel",)),
        name="final_ln",
    )(h1, y_t, ln_g.reshape(1, D), ln_b.reshape(1, D))


def _mixer(h, positions, w_in, conv_w, conv_b, rg_w_a, rg_b_a, rg_w_x, rg_b_x, lru_lambda,
           gn_attn, gn_rec, w_out, ln1_g, ln1_b, depth):
    B, S, D = h.shape
    half = HEAD_DIM // 2
    inv = ROPE_THETA ** (-jnp.arange(half, dtype=F32) / half)
    inv_full = jnp.concatenate([inv, inv]).reshape(1, HEAD_DIM)
    w_qkv = w_in[:, :3 * ATTN_WIDTH].astype(BF16)
    w_rec = w_in[:, 3 * ATTN_WIDTH:].astype(BF16)
    wax = jnp.concatenate([rg_w_a, rg_w_x], axis=-1).astype(BF16)
    q, k, v = _qkv_call(h, positions.reshape(B, S, 1), inv_full, w_qkv)
    rec_n = _rec_call(h, w_rec, conv_w, conv_b, wax, rg_b_a, rg_b_x, lru_lambda, gn_rec)
    attn = _attn_call(q, k, v)
    return _mix_call(h, attn, rec_n, w_out.astype(BF16), gn_attn, ln1_g, ln1_b, depth)


def _peer(h, hb, wq, keys1, keys2, u, v, ln2_g, ln2_b, depth):
    B, S, D = h.shape
    h2 = h.reshape(B * S, D)
    hb2 = hb.reshape(B * S, D)
    wq_t = wq.reshape(D, PEER_HEADS, 2 * PEER_HALF).transpose(1, 2, 0).astype(BF16)
    rank2, e2, coef, cnt = _router_call(hb2, wq_t, keys1.astype(BF16), keys2.astype(BF16))
    ub, vt = _uvprep_call(u, v)
    y_t = _peer_call(hb2, ub, vt, rank2, e2, coef, cnt)
    return _final_call(h2, y_t, ln2_g, ln2_b, depth).reshape(B, S, D)


def kernel(x, positions, w_in, conv_w, conv_b, rg_w_a, rg_b_a, rg_w_x, rg_b_x, lru_lambda, gn_attn, gn_rec,
           w_out, ln1_g, ln1_b, peer_wq, peer_keys1, peer_keys2, peer_u, peer_v, ln2_g, ln2_b):
    depth = w_in.shape[0]
    h = x
    for l in range(depth):
        h, hb = _mixer(h, positions, w_in[l], conv_w[l], conv_b[l], rg_w_a[l], rg_b_a[l], rg_w_x[l], rg_b_x[l],
                       lru_lambda[l], gn_attn[l], gn_rec[l], w_out[l], ln1_g[l], ln1_b[l], depth)
        h = _peer(h, hb, peer_wq[l], peer_keys1[l], peer_keys2[l], peer_u[l], peer_v[l], ln2_g[l], ln2_b[l],
                  depth)
    return h
```

```python
import functools
import math

import jax
import jax.numpy as jnp
import numpy as np
from jax import lax
from jax.experimental import pallas as pl
from jax.experimental.pallas import tpu as pltpu

F32 = jnp.float32
BF16 = jnp.bfloat16

ATTN_HEADS = 8
HEAD_DIM = 128
ATTN_WIDTH = ATTN_HEADS * HEAD_DIM
REC_BLOCKS = 8
REC_BLOCK = 128
REC_WIDTH = REC_BLOCKS * REC_BLOCK
CONV_WIDTH = 4
LRU_C = 8.0
BAND = 128
DILATIONS = (1, 4, 16)
ROPE_THETA = 10000.0
NEG_INF = -1e30
PEER_HEADS = 8
PEER_NKEYS = 128
PEER_EXPERTS = PEER_NKEYS * PEER_NKEYS
PEER_HALF = 128
PEER_TOPK = 16
SUBLANES = 8

V7X_VMEM_LIMIT = 56 * 1024 * 1024


def _params(sem, vmem=V7X_VMEM_LIMIT):
    return pltpu.CompilerParams(dimension_semantics=sem, vmem_limit_bytes=vmem)


def _resident(shape):
    zeros = (0,) * len(shape)
    return pl.BlockSpec(shape, lambda *_: zeros, pipeline_mode=pl.Buffered(1))


def _exact(v):
    assert float(np.float32(v)) == float(v)
    return jnp.full((1, 1), v, F32)


_PI_HI = float(np.float32(math.pi))
_PI_MID = float(np.float32(math.pi - _PI_HI))
_PI_LO = float(np.float32(math.pi - _PI_HI - _PI_MID))


def _ln_eps():
    return _exact(1.0) / _exact(1e5)


def _gelu_tanh(x):
    pi = _exact(_PI_HI) + _exact(_PI_MID) + _exact(_PI_LO)
    c = jnp.sqrt(_exact(2.0) / pi).astype(x.dtype)
    k = (_exact(44715.0) / _exact(1e6)).astype(x.dtype)
    return 0.5 * x * (1.0 + jnp.tanh(c * (x + k * (x * x * x))))


def _sigmoid(x):
    return 1.0 / (1.0 + jnp.exp(-x))


def _rms_scale(x):
    return lax.rsqrt(jnp.mean(x * x, axis=-1, keepdims=True) + _ln_eps())


def _layer_norm(z, g, b):
    mu = jnp.mean(z, axis=-1, keepdims=True)
    zc = z - mu
    var = jnp.mean(zc * zc, axis=-1, keepdims=True)
    return zc * lax.rsqrt(var + _ln_eps()) * g + b


def _deepnorm_alpha(depth):
    return jnp.sqrt(jnp.sqrt(_exact(2.0 * depth)))


QKV_TILE = 512


def _qkv_kernel(x_ref, pos_ref, inv_ref, w_ref, q_ref, k_ref, v_ref):
    proj = jnp.dot(x_ref[0].astype(BF16), w_ref[...], preferred_element_type=F32)
    ang = pos_ref[0].astype(F32) * inv_ref[...]
    lane = lax.broadcasted_iota(jnp.int32, ang.shape, 1)
    cos = jnp.cos(ang)
    sin = jnp.where(lane < HEAD_DIM // 2, -jnp.sin(ang), jnp.sin(ang))
    scale = lax.rsqrt(_exact(float(HEAD_DIM)))
    for h in range(ATTN_HEADS):
        cols = slice(h * HEAD_DIM, (h + 1) * HEAD_DIM)
        qh = proj[:, cols]
        kh = proj[:, ATTN_WIDTH + h * HEAD_DIM: ATTN_WIDTH + (h + 1) * HEAD_DIM]
        q_ref[0, :, cols] = (qh * cos + pltpu.roll(qh, HEAD_DIM // 2, 1) * sin) * scale
        k_ref[0, :, cols] = kh * cos + pltpu.roll(kh, HEAD_DIM // 2, 1) * sin
    v_ref[0] = proj[:, 2 * ATTN_WIDTH:]


def _qkv_call(x, pos, inv_full, w_qkv):
    B, S, D = x.shape
    tm = QKV_TILE
    spec = pl.BlockSpec((1, tm, ATTN_WIDTH), lambda b, s: (b, s, 0))
    shape = jax.ShapeDtypeStruct((B, S, ATTN_WIDTH), F32)
    return pl.pallas_call(
        _qkv_kernel,
        grid=(B, S // tm),
        in_specs=[pl.BlockSpec((1, tm, D), lambda b, s: (b, s, 0)),
                  pl.BlockSpec((1, tm, 1), lambda b, s: (b, s, 0)),
                  _resident((1, HEAD_DIM)),
                  _resident(w_qkv.shape)],
        out_specs=[spec, spec, spec],
        out_shape=[shape, shape, shape],
        compiler_params=_params(("parallel", "parallel")),
        name="qkv_rope",
    )(x, pos, inv_full, w_qkv)


REC_TILE = 512


def _rec_kernel(x_ref, w_ref, cw_ref, cb_ref, wax_ref, ba_ref, bx_ref, lam_ref, g_ref, o_ref,
                ext_ref, a_ref, b_ref, h_ref, hc_ref):
    tm = REC_TILE

    @pl.when(pl.program_id(1) == 0)
    def _():
        ext_ref[0:SUBLANES, :] = jnp.zeros((SUBLANES, REC_WIDTH), F32)
        hc_ref[...] = jnp.zeros((SUBLANES, REC_WIDTH), F32)

    proj = jnp.dot(x_ref[0].astype(BF16), w_ref[...], preferred_element_type=F32)
    xr = proj[:, :REC_WIDTH]
    gr = proj[:, REC_WIDTH:]
    ext_ref[SUBLANES:SUBLANES + tm, :] = xr
    xc = cb_ref[...] + cw_ref[CONV_WIDTH - 1:CONV_WIDTH, :] * xr
    for j in range(CONV_WIDTH - 1):
        back = CONV_WIDTH - 1 - j
        xc = xc + cw_ref[j:j + 1, :] * ext_ref[pl.ds(SUBLANES - back, tm), :]
    ext_ref[0:SUBLANES, :] = xr[tm - SUBLANES:, :]

    rs, is_ = [], []
    for n in range(REC_BLOCKS):
        xb = xc[:, n * REC_BLOCK:(n + 1) * REC_BLOCK].astype(BF16)
        gts = jnp.dot(xb, wax_ref[n], preferred_element_type=F32)
        rs.append(gts[:, :REC_BLOCK])
        is_.append(gts[:, REC_BLOCK:])
    r = _sigmoid(jnp.concatenate(rs, axis=1) + ba_ref[...])
    ig = _sigmoid(jnp.concatenate(is_, axis=1) + bx_ref[...])
    z = -lam_ref[...]
    softplus = jnp.maximum(z, 0.0) + jnp.log(1.0 + jnp.exp(-jnp.abs(z)))
    log_a = (-LRU_C) * r * softplus
    a = jnp.exp(log_a)
    bt = jnp.sqrt(1.0 - jnp.exp(2.0 * log_a)) * (ig * xc)

    r8 = lax.broadcasted_iota(jnp.int32, a.shape, 0) & (SUBLANES - 1)
    for s in (1, 2, 4):
        a_sh = pltpu.roll(a, s, 0)
        b_sh = pltpu.roll(bt, s, 0)
        m = r8 >= s
        bt = jnp.where(m, a * b_sh + bt, bt)
        a = jnp.where(m, a * a_sh, a)
    a_ref[...] = a
    b_ref[...] = bt

    def group(g, hc):
        rows = pl.ds(pl.multiple_of(g * SUBLANES, SUBLANES), SUBLANES)
        h = a_ref[rows, :] * hc + b_ref[rows, :]
        h_ref[rows, :] = h
        return jnp.broadcast_to(h[SUBLANES - 1:SUBLANES, :], (SUBLANES, REC_WIDTH))

    hc_ref[...] = lax.fori_loop(0, tm // SUBLANES, group, hc_ref[...], unroll=8)

    rec = h_ref[...] * _gelu_tanh(gr)
    o_ref[0] = (rec * _rms_scale(rec) * g_ref[...]).astype(BF16)


def _rec_call(x, w_rec, conv_w, conv_b, wax, b_a, b_x, lam, gn_rec):
    B, S, D = x.shape
    tm = REC_TILE
    row = lambda a: a.reshape(1, REC_WIDTH)
    return pl.pallas_call(
        _rec_kernel,
        grid=(B, S // tm),
        in_specs=[pl.BlockSpec((1, tm, D), lambda b, s: (b, s, 0)),
                  _resident(w_rec.shape),
                  _resident((CONV_WIDTH, REC_WIDTH)),
                  _resident((1, REC_WIDTH)),
                  _resident(wax.shape),
                  _resident((1, REC_WIDTH)), _resident((1, REC_WIDTH)),
                  _resident((1, REC_WIDTH)), _resident((1, REC_WIDTH))],
        out_specs=pl.BlockSpec((1, tm, REC_WIDTH), lambda b, s: (b, s, 0)),
        out_shape=jax.ShapeDtypeStruct((B, S, REC_WIDTH), BF16),
        scratch_shapes=[pltpu.VMEM((SUBLANES + tm, REC_WIDTH), F32),
                        pltpu.VMEM((tm, REC_WIDTH), F32),
                        pltpu.VMEM((tm, REC_WIDTH), F32),
                        pltpu.VMEM((tm, REC_WIDTH), F32),
                        pltpu.VMEM((SUBLANES, REC_WIDTH), F32)],
        compiler_params=_params(("parallel", "arbitrary")),
        name="rec_branch",
    )(x, w_rec, conv_w, row(conv_b), wax, row(b_a), row(b_x), row(lam), row(gn_rec))


ATTN_UNROLL = 16


def _band_block(q, k, v, valid):
    s = lax.dot_general(q.astype(BF16), k.astype(BF16), (((1,), (1,)), ((), ())), preferred_element_type=F32)
    s = jnp.where(valid, s, NEG_INF)
    m = jnp.max(s, axis=-1, keepdims=True)
    p = jnp.exp(s - m)
    l = jnp.sum(p, axis=-1, keepdims=True)
    acc = jnp.dot(p.astype(BF16), v.astype(BF16), preferred_element_type=F32)
    return acc, m, l


def _attn_kernel(q_ref, k_ref, v_ref, *rest, seq, with_tables):
    if with_tables:
        u_ref, vtab_ref, o_ref, ub_ref, vt_ref, *stats = rest
        ub_ref[...] = u_ref[...].astype(BF16)
        vt_ref[...] = vtab_ref[...].T.astype(BF16)
    else:
        o_ref, *stats = rest
    wide = (BAND, HEAD_DIM)

    for p, dil in enumerate(DILATIONS):
        acc_ref, m_ref, l_ref = stats[3 * p: 3 * p + 3]
        n_blk = seq // (dil * BAND)

        def block(idx, carry, dil=dil, n_blk=n_blk, acc_ref=acc_ref, m_ref=m_ref, l_ref=l_ref):
            c = idx // n_blk
            j = idx - c * n_blk
            k0 = jnp.maximum(j - 1, 0) * BAND
            n_keys = 2 * BAND if n_blk > 1 else BAND
            if dil == 1:
                rows = pl.ds(pl.multiple_of(j * BAND, BAND), BAND)
                krows = pl.ds(pl.multiple_of(k0, BAND), n_keys)
            else:
                rows = pl.ds(dil * j * BAND + c, BAND, stride=dil)
                krows = pl.ds(dil * k0 + c, n_keys, stride=dil)
            qi = lax.broadcasted_iota(jnp.int32, (BAND, n_keys), 0)
            kj = lax.broadcasted_iota(jnp.int32, (BAND, n_keys), 1)
            dist = (j * BAND + qi) - (k0 + kj)
            acc, m, l = _band_block(q_ref[0, rows, :], k_ref[0, krows, :], v_ref[0, krows, :],
                                    (dist >= 0) & (dist <= BAND))
            acc_ref[rows, :] = acc
            m_ref[rows, :] = jnp.broadcast_to(m, wide)
            l_ref[rows, :] = jnp.broadcast_to(l, wide)
            return carry

        lax.fori_loop(0, seq // BAND, block, 0, unroll=ATTN_UNROLL)

    def merge(j, carry):
        rows = pl.ds(pl.multiple_of(j * BAND, BAND), BAND)
        ms = [stats[3 * p + 1][rows, :] for p in range(3)]
        mx = jnp.maximum(jnp.maximum(ms[0], ms[1]), ms[2])
        ws = [jnp.exp(mm - mx) for mm in ms]
        num = sum(ws[p] * stats[3 * p][rows, :] for p in range(3))
        den = sum(ws[p] * stats[3 * p + 2][rows, :] for p in range(3))
        o_ref[0, rows, :] = num / den
        return carry

    lax.fori_loop(0, seq // BAND, merge, 0, unroll=ATTN_UNROLL)


ATTN_TABLE_ROWS_MAX = 512


def _attn_call(q, k, v, u, v_tab):
    B, S, _ = q.shape
    assert S % (max(DILATIONS) * BAND) == 0
    E, D = u.shape
    te = E // (B * ATTN_HEADS)
    with_tables = te * B * ATTN_HEADS == E and te % HEAD_DIM == 0 and te <= ATTN_TABLE_ROWS_MAX
    spec = pl.BlockSpec((1, S, HEAD_DIM), lambda b, h: (b, 0, h))
    stat = pltpu.VMEM((S, HEAD_DIM), F32)
    in_specs, out_specs = [spec] * 3, [spec]
    out_shape = [jax.ShapeDtypeStruct((B, S, ATTN_WIDTH), F32)]
    args = (q, k, v)
    if with_tables:
        rows = pl.BlockSpec((te, D), lambda b, h: (b * ATTN_HEADS + h, 0))
        in_specs += [rows, rows]
        out_specs += [rows, pl.BlockSpec((D, te), lambda b, h: (0, b * ATTN_HEADS + h))]
        out_shape += [jax.ShapeDtypeStruct((E, D), BF16), jax.ShapeDtypeStruct((D, E), BF16)]
        args += (u, v_tab)
    outs = pl.pallas_call(
        functools.partial(_attn_kernel, seq=S, with_tables=with_tables),
        grid=(B, ATTN_HEADS),
        in_specs=in_specs,
        out_specs=out_specs,
        out_shape=out_shape,
        scratch_shapes=[stat] * 9,
        compiler_params=_params(("parallel", "parallel")),
        name="dilated_attn",
    )(*args)
    return tuple(outs) if with_tables else (outs[0], None, None)


MIX_TILE = 512


def _mix_kernel(x_ref, at_ref, rc_ref, wo_ref, ga_ref, g_ref, b_ref, o_ref, ob_ref, *, depth):
    at = at_ref[0]
    at_n = (at * _rms_scale(at) * ga_ref[...]).astype(BF16)
    y = jnp.dot(at_n, wo_ref[:ATTN_WIDTH, :], preferred_element_type=F32)
    y = y + jnp.dot(rc_ref[0], wo_ref[ATTN_WIDTH:, :], preferred_element_type=F32)
    h = _layer_norm(_deepnorm_alpha(depth) * x_ref[0] + y, g_ref[...], b_ref[...])
    o_ref[0] = h
    ob_ref[0] = h.astype(BF16)


def _mix_call(x, attn, rec_n, w_out, gn_attn, ln_g, ln_b, depth):
    B, S, D = x.shape
    tm = MIX_TILE
    row = lambda a, n: a.reshape(1, n)
    tile = lambda n: pl.BlockSpec((1, tm, n), lambda b, s: (b, s, 0))
    return pl.pallas_call(
        functools.partial(_mix_kernel, depth=depth),
        grid=(B, S // tm),
        in_specs=[tile(D), tile(ATTN_WIDTH), tile(REC_WIDTH),
                  _resident(w_out.shape),
                  _resident((1, ATTN_WIDTH)), _resident((1, D)), _resident((1, D))],
        out_specs=[tile(D), tile(D)],
        out_shape=[jax.ShapeDtypeStruct((B, S, D), F32), jax.ShapeDtypeStruct((B, S, D), BF16)],
        compiler_params=_params(("parallel", "parallel")),
        name="mix_out_ln",
    )(x, attn, rec_n, w_out, row(gn_attn, ATTN_WIDTH), row(ln_g, D), row(ln_b, D))


ROUTER_TILE = 1024
N_CAND_ROWS = 16 + 7 * 8 + 8
ROUTER_HEADS_PER_TRIP = 1


def _sort_network(n):
    def merge(lo, hi, r):
        step = r * 2
        if step < hi - lo:
            yield from merge(lo, hi, step)
            yield from merge(lo + r, hi, step)
            yield from ((i, i + r) for i in range(lo + r, hi - r, step))
        else:
            yield (lo, lo + r)

    def sort(lo, hi):
        if hi - lo >= 1:
            mid = lo + (hi - lo) // 2
            yield from sort(lo, mid)
            yield from sort(mid + 1, hi)
            yield from merge(lo, hi, 1)

    return tuple(sort(0, n - 1))


_SORT16 = _sort_network(PEER_TOPK)
_BITONIC16 = tuple((i, i + d) for d in (8, 4, 2, 1) for i in range(PEER_TOPK) if i & d == 0)


def _exchange(x, pairs):
    for i, j in pairs:
        a, b = x[i], x[j]
        if a is None:
            x[i], x[j] = b, None
        elif b is not None:
            x[i], x[j] = jnp.maximum(a, b), jnp.minimum(a, b)
    return x


def _top16_sorted(blocks):
    x = _exchange(list(blocks) + [None] * (PEER_TOPK - len(blocks)), _SORT16)
    for shift in (4, 2, 1):
        partner = [None if v is None else pltpu.roll(v, shift, 0) for v in x]
        merged = []
        for i in range(PEER_TOPK):
            a, b = x[i], partner[PEER_TOPK - 1 - i]
            merged.append(b if a is None else a if b is None else jnp.maximum(a, b))
        x = _exchange(merged, _BITONIC16)
    return x


def _sublane_blocks(a):
    return [a[i * SUBLANES:(i + 1) * SUBLANES] for i in range(a.shape[0] // SUBLANES)]


def _count(flags):
    total = sum(jnp.where(f, 1.0, 0.0) for f in flags)
    return jnp.sum(total, axis=0, keepdims=True)


def _route_head(s1, s2):
    b1, b2 = _sublane_blocks(s1), _sublane_blocks(s2)
    v1, v2 = _top16_sorted(b1), _top16_sorted(b2)
    rows = (v1[0].shape[0], v1[0].shape[1])
    sub = lax.broadcasted_iota(jnp.int32, rows, 0)

    def on_sublanes(vals):
        out = vals[0]
        for j in range(1, SUBLANES):
            out = jnp.where(sub == j, vals[j], out)
        return out

    v2_lo, v2_hi, v1_hi = on_sublanes(v2[:8]), on_sublanes(v2[8:]), on_sublanes(v1[8:])
    cand = [v1[0] + v2_lo, v1[0] + v2_hi] + [v1[r] + v2_lo for r in range(1, 8)] + [v1_hi + v2[0]]
    best = _top16_sorted(cand)
    chosen = [c >= best[PEER_TOPK - 1] for c in cand]
    picked = [jnp.where(f, 1.0, 0.0) for f in chosen]
    z = jnp.sum(sum(p * jnp.exp(c - best[0]) for p, c in zip(picked, cand)), axis=0, keepdims=True)
    cnt_r = [jnp.sum(picked[0] + picked[1], axis=0, keepdims=True)]
    cnt_r += [jnp.sum(picked[1 + r], axis=0, keepdims=True) for r in range(1, 8)]
    cnt_r += [picked[9][j:j + 1] for j in range(SUBLANES)]

    cnt_r = [jnp.broadcast_to(c, rows) for c in cnt_r]
    rank2, cnt_k, coef, e2 = [], [], [], []
    inv_z = 1.0 / z
    for blk1, blk2 in zip(b1, b2):
        r2 = jnp.full(rows, float(PEER_TOPK), F32)
        ck = jnp.zeros(rows, F32)
        for r in range(PEER_TOPK - 1, -1, -1):
            r2 = jnp.where(blk2 >= v2[r], float(r), r2)
            ck = jnp.where(blk1 == v1[r], cnt_r[r], ck)
        rank2.append(r2)
        cnt_k.append(ck)
        coef.append(jnp.exp(blk1 - v1[0]) * inv_z)
        e2.append(jnp.exp(blk2 - v2[0]))

    strict = [v[r] > v[r + 1] for v in (v1, v2) for r in range(PEER_TOPK - 1)]
    ok = _count(strict) == float(SUBLANES * len(strict))
    for blocks, vals in ((b1, v1), (b2, v2), (cand, best)):
        ok = ok & (_count([b >= vals[PEER_TOPK - 1] for b in blocks]) == float(PEER_TOPK))
    redo = jnp.max(jnp.where(ok, 0.0, 1.0))
    cat = lambda parts: jnp.concatenate(parts, axis=0)
    return cat(rank2), cat(e2), cat(coef), cat(cnt_k), redo


def _top16_rows_any(s):
    idx = lax.broadcasted_iota(jnp.int32, s.shape, 0)
    rank = jnp.full(s.shape, float(PEER_TOPK), F32)
    vals = []
    for r in range(PEER_TOPK):
        m = jnp.max(s, axis=0, keepdims=True)
        sel = idx == jnp.min(jnp.where(s == m, idx, s.shape[0]), axis=0, keepdims=True)
        rank = jnp.where(sel, float(r), rank)
        s = jnp.where(sel, -jnp.inf, s)
        vals.append(m)
    return rank, jnp.concatenate(vals, axis=0)


def _route_head_any(s1, s2):
    rank1, v1 = _top16_rows_any(s1)
    rank2, v2 = _top16_rows_any(s2)
    blocks = [v1[0:1] + v2]
    for r1 in range(1, 8):
        blocks.append(v1[r1:r1 + 1] + v2[0:8])
    blocks.append(v1[8:16] + v2[0:1])
    cand = jnp.concatenate(blocks, axis=0)
    rank3, _ = _top16_rows_any(cand)
    chosen = jnp.where(rank3 < PEER_TOPK, 1.0, 0.0)
    z = jnp.sum(chosen * jnp.exp(cand - cand[0:1]), axis=0, keepdims=True)
    counts = [jnp.sum(chosen[0:16], axis=0, keepdims=True)]
    for r1 in range(1, 8):
        counts.append(jnp.sum(chosen[8 + 8 * r1: 16 + 8 * r1], axis=0, keepdims=True))
    counts.append(chosen[N_CAND_ROWS - 8:])
    cnt_r = jnp.concatenate(counts, axis=0)
    cnt_k = jnp.zeros(s1.shape, F32)
    for r in range(PEER_TOPK):
        cnt_k = jnp.where(rank1 == r, cnt_r[r:r + 1], cnt_k)
    return rank2, jnp.exp(s2 - v2[0:1]), jnp.exp(s1 - v1[0:1]) / z, cnt_k


def _router_kernel(h_ref, wq_ref, k1_ref, k2_ref, rank2_ref, e2_ref, coef_ref, cnt_ref):
    hb = h_ref[...]

    def emit(h, rank2, e2, coef, cnt_k):
        rank2_ref[h] = rank2.astype(BF16)
        e2_ref[h] = e2.astype(BF16)
        coef_ref[h] = coef
        cnt_ref[h] = cnt_k

    def heads(g, carry):
        todo = []
        for i in range(ROUTER_HEADS_PER_TRIP):
            h = g * ROUTER_HEADS_PER_TRIP + i
            q_t = lax.dot_general(wq_ref[h], hb, (((1,), (1,)), ((), ())), preferred_element_type=F32)
            s1 = jnp.dot(k1_ref[...], q_t[:PEER_HALF].astype(BF16), preferred_element_type=F32)
            s2 = jnp.dot(k2_ref[...], q_t[PEER_HALF:].astype(BF16), preferred_element_type=F32)
            *stats, redo = _route_head(s1, s2)
            emit(h, *stats)
            todo.append((h, s1, s2, redo))
        for h, s1, s2, redo in todo:
            pl.when(redo > 0.0)(functools.partial(lambda h, s1, s2: emit(h, *_route_head_any(s1, s2)), h, s1, s2))
        return carry

    lax.fori_loop(0, PEER_HEADS // ROUTER_HEADS_PER_TRIP, heads, 0)


def _router_call(hb, wq_t, k1, k2):
    T, D = hb.shape
    tm = ROUTER_TILE
    o_spec = pl.BlockSpec((PEER_HEADS, PEER_NKEYS, tm), lambda t: (0, 0, t))
    shape = lambda dt: jax.ShapeDtypeStruct((PEER_HEADS, PEER_NKEYS, T), dt)
    return pl.pallas_call(
        _router_kernel,
        grid=(T // tm,),
        in_specs=[pl.BlockSpec((tm, D), lambda t: (t, 0)),
                  _resident(wq_t.shape), _resident(k1.shape), _resident(k2.shape)],
        out_specs=[o_spec] * 4,
        out_shape=[shape(BF16), shape(BF16), shape(F32), shape(F32)],
        compiler_params=_params(("parallel",)),
        name="peer_router",
    )(hb, wq_t, k1, k2)


PREP_TILE = 512


def _uvprep_kernel(u_ref, v_ref, ub_ref, vt_ref):
    ub_ref[...] = u_ref[...].astype(BF16)
    vt_ref[...] = v_ref[...].T.astype(BF16)


def _uvprep_call(u, v):
    E, D = u.shape
    te = PREP_TILE
    return pl.pallas_call(
        _uvprep_kernel,
        grid=(E // te,),
        in_specs=[pl.BlockSpec((te, D), lambda e: (e, 0)), pl.BlockSpec((te, D), lambda e: (e, 0))],
        out_specs=[pl.BlockSpec((te, D), lambda e: (e, 0)), pl.BlockSpec((D, te), lambda e: (0, e))],
        out_shape=[jax.ShapeDtypeStruct((E, D), BF16), jax.ShapeDtypeStruct((D, E), BF16)],
        compiler_params=_params(("parallel",)),
        name="peer_uvprep",
    )(u, v)


PEER_TOK_TILE = 512
PEER_EXP_TILE = 1024
KEYS_PER_STEP = PEER_EXP_TILE // PEER_NKEYS
PEER_STAGES = 3


def _peer_kernel(h_ref, u_ref, vt_ref, rank2_ref, e2_ref, coef_ref, cnt_ref, o_ref, act_ref, p_ref, *, n_exp_tiles):
    s = pl.program_id(0)
    cur = s % 2
    prev = (s + 1) % 2

    @pl.when(s == 0)
    def _():
        act_ref[1] = jnp.zeros(act_ref.shape[1:], BF16)
        p_ref[0] = jnp.zeros(p_ref.shape[1:], BF16)

    @pl.when(jnp.maximum(s - (PEER_STAGES - 1), 0) % n_exp_tiles == 0)
    def _():
        o_ref[...] = jnp.zeros(o_ref.shape, F32)

    act_ref[cur] = lax.dot_general(u_ref[...], h_ref[...], (((1,), (1,)), ((), ())),
                                   preferred_element_type=F32).astype(BF16)

    for key in range(KEYS_PER_STEP):
        gate = None
        for h in range(PEER_HEADS):
            cnt = cnt_ref[h, key:key + 1, :].astype(BF16)
            cf = coef_ref[h, key:key + 1, :].astype(BF16)
            term = jnp.where(rank2_ref[h] < cnt, e2_ref[h], jnp.zeros((), BF16)) * cf
            gate = term if gate is None else gate + term
        rows = slice(key * PEER_NKEYS, (key + 1) * PEER_NKEYS)
        p_ref[prev, rows, :] = gate * _gelu_tanh(act_ref[prev, rows, :])

    o_ref[...] += jnp.dot(vt_ref[...], p_ref[cur], preferred_element_type=F32)


def _peer_call(hb, ub, vt, rank2, e2, coef, cnt):
    T, D = hb.shape
    tm, te = PEER_TOK_TILE, PEER_EXP_TILE
    n_e = PEER_EXPERTS // te
    n_pairs = (T // tm) * n_e
    by_tile = lambda a: a.reshape(PEER_HEADS, n_e, KEYS_PER_STEP, T)

    def pair(s, lag):
        i = jnp.clip(s - lag, 0, n_pairs - 1)
        return i // n_e, i % n_e

    head_spec = pl.BlockSpec((PEER_HEADS, PEER_NKEYS, tm), lambda s: (0, 0, pair(s, 1)[0]))
    key_spec = pl.BlockSpec((PEER_HEADS, None, KEYS_PER_STEP, tm),
                            lambda s: (0, pair(s, 1)[1], 0, pair(s, 1)[0]))
    return pl.pallas_call(
        functools.partial(_peer_kernel, n_exp_tiles=n_e),
        grid=(n_pairs + PEER_STAGES - 1,),
        in_specs=[pl.BlockSpec((tm, D), lambda s: (pair(s, 0)[0], 0)),
                  pl.BlockSpec((te, D), lambda s: (pair(s, 0)[1], 0)),
                  pl.BlockSpec((D, te), lambda s: (0, pair(s, 2)[1])),
                  head_spec, head_spec, key_spec, key_spec],
        out_specs=pl.BlockSpec((D, tm), lambda s: (0, pair(s, 2)[0])),
        out_shape=jax.ShapeDtypeStruct((D, T), F32),
        scratch_shapes=[pltpu.VMEM((2, te, tm), BF16), pltpu.VMEM((2, te, tm), BF16)],
        compiler_params=_params(("arbitrary",)),
        name="peer_dense",
    )(hb, ub, vt, rank2, e2, by_tile(coef), by_tile(cnt))


FINAL_TILE = 512


def _final_kernel(h_ref, yt_ref, g_ref, b_ref, o_ref, *, depth):
    o_ref[...] = _layer_norm(_deepnorm_alpha(depth) * h_ref[...] + yt_ref[...].T, g_ref[...], b_ref[...])


def _final_call(h1, y_t, ln_g, ln_b, depth):
    T, D = h1.shape
    tm = FINAL_TILE
    return pl.pallas_call(
        functools.partial(_final_kernel, depth=depth),
        grid=(T // tm,),
        in_specs=[pl.BlockSpec((tm, D), lambda t: (t, 0)),
                  pl.BlockSpec((D, tm), lambda t: (0, t)),
                  _resident((1, D)), _resident((1, D))],
        out_specs=pl.BlockSpec((tm, D), lambda t: (t, 0)),
        out_shape=jax.ShapeDtypeStruct((T, D), F32),
        compiler_params=_params(("parallel",)),
        name="final_ln",
    )(h1, y_t, ln_g.reshape(1, D), ln_b.reshape(1, D))


def _mixer(h, positions, w_in, conv_w, conv_b, rg_w_a, rg_b_a, rg_w_x, rg_b_x, lru_lambda,
           gn_attn, gn_rec, w_out, ln1_g, ln1_b, peer_u, peer_v, depth):
    B, S, D = h.shape
    half = HEAD_DIM // 2
    inv = ROPE_THETA ** (-jnp.arange(half, dtype=F32) / half)
    inv_full = jnp.concatenate([inv, inv]).reshape(1, HEAD_DIM)
    w_qkv = w_in[:, :3 * ATTN_WIDTH].astype(BF16)
    w_rec = w_in[:, 3 * ATTN_WIDTH:].astype(BF16)
    wax = jnp.concatenate([rg_w_a, rg_w_x], axis=-1).astype(BF16)
    q, k, v = _qkv_call(h, positions.reshape(B, S, 1), inv_full, w_qkv)
    rec_n = _rec_call(h, w_rec, conv_w, conv_b, wax, rg_b_a, rg_b_x, lru_lambda, gn_rec)
    attn, ub, vt = _attn_call(q, k, v, peer_u, peer_v)
    h1, h1b = _mix_call(h, attn, rec_n, w_out.astype(BF16), gn_attn, ln1_g, ln1_b, depth)
    return h1, h1b, ub, vt


def _peer(h, hb, wq, keys1, keys2, u, v, ub, vt, ln2_g, ln2_b, depth):
    B, S, D = h.shape
    h2 = h.reshape(B * S, D)
    hb2 = hb.reshape(B * S, D)
    wq_t = wq.reshape(D, PEER_HEADS, 2 * PEER_HALF).transpose(1, 2, 0).astype(BF16)
    rank2, e2, coef, cnt = _router_call(hb2, wq_t, keys1.astype(BF16), keys2.astype(BF16))
    if ub is None:
        ub, vt = _uvprep_call(u, v)
    y_t = _peer_call(hb2, ub, vt, rank2, e2, coef, cnt)
    return _final_call(h2, y_t, ln2_g, ln2_b, depth).reshape(B, S, D)


def kernel(x, positions, w_in, conv_w, conv_b, rg_w_a, rg_b_a, rg_w_x, rg_b_x, lru_lambda, gn_attn, gn_rec,
           w_out, ln1_g, ln1_b, peer_wq, peer_keys1, peer_keys2, peer_u, peer_v, ln2_g, ln2_b):
    depth = w_in.shape[0]
    h = x
    for l in range(depth):
        h, hb, ub, vt = _mixer(h, positions, w_in[l], conv_w[l], conv_b[l], rg_w_a[l], rg_b_a[l], rg_w_x[l],
                               rg_b_x[l], lru_lambda[l], gn_attn[l], gn_rec[l], w_out[l], ln1_g[l], ln1_b[l],
                               peer_u[l], peer_v[l], depth)
        h = _peer(h, hb, peer_wq[l], peer_keys1[l], peer_keys2[l], peer_u[l], peer_v[l], ub, vt, ln2_g[l],
                  ln2_b[l], depth)
    return h
```

```python
import functools
import math

import jax
import jax.numpy as jnp
import numpy as np
from jax import lax
from jax.experimental import pallas as pl
from jax.experimental.pallas import tpu as pltpu

F32 = jnp.float32
BF16 = jnp.bfloat16

ATTN_HEADS = 8
HEAD_DIM = 128
ATTN_WIDTH = ATTN_HEADS * HEAD_DIM
REC_BLOCKS = 8
REC_BLOCK = 128
REC_WIDTH = REC_BLOCKS * REC_BLOCK
CONV_WIDTH = 4
LRU_C = 8.0
BAND = 128
DILATIONS = (1, 4, 16)
ROPE_THETA = 10000.0
NEG_INF = -1e30
PEER_HEADS = 8
PEER_NKEYS = 128
PEER_EXPERTS = PEER_NKEYS * PEER_NKEYS
PEER_HALF = 128
PEER_TOPK = 16
SUBLANES = 8

V7X_VMEM_LIMIT = 56 * 1024 * 1024


def _params(sem, vmem=V7X_VMEM_LIMIT):
    return pltpu.CompilerParams(dimension_semantics=sem, vmem_limit_bytes=vmem)


def _resident(shape):
    zeros = (0,) * len(shape)
    return pl.BlockSpec(shape, lambda *_: zeros, pipeline_mode=pl.Buffered(1))


def _exact(v):
    assert float(np.float32(v)) == float(v)
    return jnp.full((1, 1), v, F32)


_PI_HI = float(np.float32(math.pi))
_PI_MID = float(np.float32(math.pi - _PI_HI))
_PI_LO = float(np.float32(math.pi - _PI_HI - _PI_MID))


def _ln_eps():
    return _exact(1.0) / _exact(1e5)


def _gelu_tanh(x):
    pi = _exact(_PI_HI) + _exact(_PI_MID) + _exact(_PI_LO)
    c = jnp.sqrt(_exact(2.0) / pi).astype(x.dtype)
    k = (_exact(44715.0) / _exact(1e6)).astype(x.dtype)
    return 0.5 * x * (1.0 + jnp.tanh(c * (x + k * (x * x * x))))


def _sigmoid(x):
    return 1.0 / (1.0 + jnp.exp(-x))


def _rms_scale(x):
    return lax.rsqrt(jnp.mean(x * x, axis=-1, keepdims=True) + _ln_eps())


def _layer_norm(z, g, b):
    mu = jnp.mean(z, axis=-1, keepdims=True)
    zc = z - mu
    var = jnp.mean(zc * zc, axis=-1, keepdims=True)
    return zc * lax.rsqrt(var + _ln_eps()) * g + b


def _deepnorm_alpha(depth):
    return jnp.sqrt(jnp.sqrt(_exact(2.0 * depth)))


QKV_TILE = 512


def _qkv_kernel(x_ref, pos_ref, inv_ref, w_ref, q_ref, k_ref, v_ref):
    proj = jnp.dot(x_ref[0].astype(BF16), w_ref[...], preferred_element_type=F32)
    ang = pos_ref[0].astype(F32) * inv_ref[...]
    lane = lax.broadcasted_iota(jnp.int32, ang.shape, 1)
    cos = jnp.cos(ang)
    sin = jnp.where(lane < HEAD_DIM // 2, -jnp.sin(ang), jnp.sin(ang))
    scale = lax.rsqrt(_exact(float(HEAD_DIM)))
    for h in range(ATTN_HEADS):
        cols = slice(h * HEAD_DIM, (h + 1) * HEAD_DIM)
        qh = proj[:, cols]
        kh = proj[:, ATTN_WIDTH + h * HEAD_DIM: ATTN_WIDTH + (h + 1) * HEAD_DIM]
        q_ref[0, :, cols] = (qh * cos + pltpu.roll(qh, HEAD_DIM // 2, 1) * sin) * scale
        k_ref[0, :, cols] = kh * cos + pltpu.roll(kh, HEAD_DIM // 2, 1) * sin
    v_ref[0] = proj[:, 2 * ATTN_WIDTH:]


def _qkv_call(x, pos, inv_full, w_qkv):
    B, S, D = x.shape
    tm = QKV_TILE
    spec = pl.BlockSpec((1, tm, ATTN_WIDTH), lambda b, s: (b, s, 0))
    shape = jax.ShapeDtypeStruct((B, S, ATTN_WIDTH), F32)
    return pl.pallas_call(
        _qkv_kernel,
        grid=(B, S // tm),
        in_specs=[pl.BlockSpec((1, tm, D), lambda b, s: (b, s, 0)),
                  pl.BlockSpec((1, tm, 1), lambda b, s: (b, s, 0)),
                  _resident((1, HEAD_DIM)),
                  _resident(w_qkv.shape)],
        out_specs=[spec, spec, spec],
        out_shape=[shape, shape, shape],
        compiler_params=_params(("parallel", "parallel")),
        name="qkv_rope",
    )(x, pos, inv_full, w_qkv)


REC_TILE = 512


def _rec_kernel(x_ref, w_ref, cw_ref, cb_ref, wax_ref, ba_ref, bx_ref, lam_ref, g_ref, o_ref,
                ext_ref, a_ref, b_ref, h_ref, hc_ref):
    tm = REC_TILE

    @pl.when(pl.program_id(1) == 0)
    def _():
        ext_ref[0:SUBLANES, :] = jnp.zeros((SUBLANES, REC_WIDTH), F32)
        hc_ref[...] = jnp.zeros((SUBLANES, REC_WIDTH), F32)

    proj = jnp.dot(x_ref[0].astype(BF16), w_ref[...], preferred_element_type=F32)
    xr = proj[:, :REC_WIDTH]
    gr = proj[:, REC_WIDTH:]
    ext_ref[SUBLANES:SUBLANES + tm, :] = xr
    xc = cb_ref[...] + cw_ref[CONV_WIDTH - 1:CONV_WIDTH, :] * xr
    for j in range(CONV_WIDTH - 1):
        back = CONV_WIDTH - 1 - j
        xc = xc + cw_ref[j:j + 1, :] * ext_ref[pl.ds(SUBLANES - back, tm), :]
    ext_ref[0:SUBLANES, :] = xr[tm - SUBLANES:, :]

    rs, is_ = [], []
    for n in range(REC_BLOCKS):
        xb = xc[:, n * REC_BLOCK:(n + 1) * REC_BLOCK].astype(BF16)
        gts = jnp.dot(xb, wax_ref[n], preferred_element_type=F32)
        rs.append(gts[:, :REC_BLOCK])
        is_.append(gts[:, REC_BLOCK:])
    r = _sigmoid(jnp.concatenate(rs, axis=1) + ba_ref[...])
    ig = _sigmoid(jnp.concatenate(is_, axis=1) + bx_ref[...])
    z = -lam_ref[...]
    softplus = jnp.maximum(z, 0.0) + jnp.log(1.0 + jnp.exp(-jnp.abs(z)))
    log_a = (-LRU_C) * r * softplus
    a = jnp.exp(log_a)
    bt = jnp.sqrt(1.0 - jnp.exp(2.0 * log_a)) * (ig * xc)

    r8 = lax.broadcasted_iota(jnp.int32, a.shape, 0) & (SUBLANES - 1)
    for s in (1, 2, 4):
        a_sh = pltpu.roll(a, s, 0)
        b_sh = pltpu.roll(bt, s, 0)
        m = r8 >= s
        bt = jnp.where(m, a * b_sh + bt, bt)
        a = jnp.where(m, a * a_sh, a)
    a_ref[...] = a
    b_ref[...] = bt

    def group(g, hc):
        rows = pl.ds(pl.multiple_of(g * SUBLANES, SUBLANES), SUBLANES)
        h = a_ref[rows, :] * hc + b_ref[rows, :]
        h_ref[rows, :] = h
        return jnp.broadcast_to(h[SUBLANES - 1:SUBLANES, :], (SUBLANES, REC_WIDTH))

    hc_ref[...] = lax.fori_loop(0, tm // SUBLANES, group, hc_ref[...], unroll=8)

    rec = h_ref[...] * _gelu_tanh(gr)
    o_ref[0] = (rec * _rms_scale(rec) * g_ref[...]).astype(BF16)


def _rec_call(x, w_rec, conv_w, conv_b, wax, b_a, b_x, lam, gn_rec):
    B, S, D = x.shape
    tm = REC_TILE
    row = lambda a: a.reshape(1, REC_WIDTH)
    return pl.pallas_call(
        _rec_kernel,
        grid=(B, S // tm),
        in_specs=[pl.BlockSpec((1, tm, D), lambda b, s: (b, s, 0)),
                  _resident(w_rec.shape),
                  _resident((CONV_WIDTH, REC_WIDTH)),
                  _resident((1, REC_WIDTH)),
                  _resident(wax.shape),
                  _resident((1, REC_WIDTH)), _resident((1, REC_WIDTH)),
                  _resident((1, REC_WIDTH)), _resident((1, REC_WIDTH))],
        out_specs=pl.BlockSpec((1, tm, REC_WIDTH), lambda b, s: (b, s, 0)),
        out_shape=jax.ShapeDtypeStruct((B, S, REC_WIDTH), BF16),
        scratch_shapes=[pltpu.VMEM((SUBLANES + tm, REC_WIDTH), F32),
                        pltpu.VMEM((tm, REC_WIDTH), F32),
                        pltpu.VMEM((tm, REC_WIDTH), F32),
                        pltpu.VMEM((tm, REC_WIDTH), F32),
                        pltpu.VMEM((SUBLANES, REC_WIDTH), F32)],
        compiler_params=_params(("parallel", "arbitrary")),
        name="rec_branch",
    )(x, w_rec, conv_w, row(conv_b), wax, row(b_a), row(b_x), row(lam), row(gn_rec))


ATTN_UNROLL = 16


def _band_block(q, k, v, valid):
    s = lax.dot_general(q.astype(BF16), k.astype(BF16), (((1,), (1,)), ((), ())), preferred_element_type=F32)
    s = jnp.where(valid, s, NEG_INF)
    m = jnp.max(s, axis=-1, keepdims=True)
    p = jnp.exp(s - m)
    l = jnp.sum(p, axis=-1, keepdims=True)
    acc = jnp.dot(p.astype(BF16), v.astype(BF16), preferred_element_type=F32)
    return acc, m, l


def _attn_kernel(q_ref, k_ref, v_ref, *rest, seq, transposed):
    n = len(transposed)
    srcs, o_ref, dsts, stats = rest[:n], rest[n], rest[n + 1:2 * n + 1], rest[2 * n + 1:]
    for src, dst, flip in zip(srcs, dsts, transposed):
        dst[...] = (src[...].T if flip else src[...]).astype(BF16)
    wide = (BAND, HEAD_DIM)

    for p, dil in enumerate(DILATIONS):
        acc_ref, m_ref, l_ref = stats[3 * p: 3 * p + 3]
        n_blk = seq // (dil * BAND)

        def block(idx, carry, dil=dil, n_blk=n_blk, acc_ref=acc_ref, m_ref=m_ref, l_ref=l_ref):
            c = idx // n_blk
            j = idx - c * n_blk
            k0 = jnp.maximum(j - 1, 0) * BAND
            n_keys = 2 * BAND if n_blk > 1 else BAND
            if dil == 1:
                rows = pl.ds(pl.multiple_of(j * BAND, BAND), BAND)
                krows = pl.ds(pl.multiple_of(k0, BAND), n_keys)
            else:
                rows = pl.ds(dil * j * BAND + c, BAND, stride=dil)
                krows = pl.ds(dil * k0 + c, n_keys, stride=dil)
            qi = lax.broadcasted_iota(jnp.int32, (BAND, n_keys), 0)
            kj = lax.broadcasted_iota(jnp.int32, (BAND, n_keys), 1)
            dist = (j * BAND + qi) - (k0 + kj)
            acc, m, l = _band_block(q_ref[0, rows, :], k_ref[0, krows, :], v_ref[0, krows, :],
                                    (dist >= 0) & (dist <= BAND))
            acc_ref[rows, :] = acc
            m_ref[rows, :] = jnp.broadcast_to(m, wide)
            l_ref[rows, :] = jnp.broadcast_to(l, wide)
            return carry

        lax.fori_loop(0, seq // BAND, block, 0, unroll=ATTN_UNROLL)

    def merge(j, carry):
        rows = pl.ds(pl.multiple_of(j * BAND, BAND), BAND)
        ms = [stats[3 * p + 1][rows, :] for p in range(3)]
        mx = jnp.maximum(jnp.maximum(ms[0], ms[1]), ms[2])
        ws = [jnp.exp(mm - mx) for mm in ms]
        num = sum(ws[p] * stats[3 * p][rows, :] for p in range(3))
        den = sum(ws[p] * stats[3 * p + 2][rows, :] for p in range(3))
        o_ref[0, rows, :] = num / den
        return carry

    lax.fori_loop(0, seq // BAND, merge, 0, unroll=ATTN_UNROLL)


ATTN_RIDER_BYTES_MAX = 4 * 1024 * 1024


class _Rider:
    def __init__(self, src, in_block, in_index, out_shape, out_block, out_index, transposed):
        self.src, self.transposed = src, transposed
        self.in_spec = pl.BlockSpec(in_block, in_index)
        self.out_spec = pl.BlockSpec(out_block, out_index)
        self.out_shape = jax.ShapeDtypeStruct(out_shape, BF16)
        rows, cols = [d for d in in_block if d is not None]
        self.fits = rows % 16 == 0 and cols % HEAD_DIM == 0 and rows * cols * 4 <= ATTN_RIDER_BYTES_MAX
        if transposed:
            self.fits = self.fits and rows % HEAD_DIM == 0


def _row_slices(w, n_steps, n_heads, transposed):
    rows, cols = w.shape
    r = rows // n_steps
    step = lambda b, h: b * n_heads + h
    if r * n_steps != rows:
        return None
    if transposed:
        return _Rider(w, (r, cols), lambda b, h: (step(b, h), 0), (cols, rows), (cols, r),
                      lambda b, h: (0, step(b, h)), True)
    return _Rider(w, (r, cols), lambda b, h: (step(b, h), 0), (rows, cols), (r, cols),
                  lambda b, h: (step(b, h), 0), False)


def _attn_call(q, k, v, riders=()):
    B, S, _ = q.shape
    assert S % (max(DILATIONS) * BAND) == 0
    spec = pl.BlockSpec((1, S, HEAD_DIM), lambda b, h: (b, 0, h))
    stat = pltpu.VMEM((S, HEAD_DIM), F32)
    outs = pl.pallas_call(
        functools.partial(_attn_kernel, seq=S, transposed=tuple(r.transposed for r in riders)),
        grid=(B, ATTN_HEADS),
        in_specs=[spec] * 3 + [r.in_spec for r in riders],
        out_specs=[spec] + [r.out_spec for r in riders],
        out_shape=[jax.ShapeDtypeStruct((B, S, ATTN_WIDTH), F32)] + [r.out_shape for r in riders],
        scratch_shapes=[stat] * 9,
        compiler_params=_params(("parallel", "parallel")),
        name="dilated_attn",
    )(q, k, v, *[r.src for r in riders])
    return tuple(outs)


MIX_TILE = 512


def _mix_kernel(x_ref, at_ref, rc_ref, wo_ref, ga_ref, g_ref, b_ref, o_ref, ob_ref, *, depth):
    at = at_ref[0]
    at_n = (at * _rms_scale(at) * ga_ref[...]).astype(BF16)
    y = jnp.dot(at_n, wo_ref[:ATTN_WIDTH, :], preferred_element_type=F32)
    y = y + jnp.dot(rc_ref[0], wo_ref[ATTN_WIDTH:, :], preferred_element_type=F32)
    h = _layer_norm(_deepnorm_alpha(depth) * x_ref[0] + y, g_ref[...], b_ref[...])
    o_ref[0] = h
    ob_ref[0] = h.astype(BF16)


def _mix_call(x, attn, rec_n, w_out, gn_attn, ln_g, ln_b, depth):
    B, S, D = x.shape
    tm = MIX_TILE
    row = lambda a, n: a.reshape(1, n)
    tile = lambda n: pl.BlockSpec((1, tm, n), lambda b, s: (b, s, 0))
    return pl.pallas_call(
        functools.partial(_mix_kernel, depth=depth),
        grid=(B, S // tm),
        in_specs=[tile(D), tile(ATTN_WIDTH), tile(REC_WIDTH),
                  _resident(w_out.shape),
                  _resident((1, ATTN_WIDTH)), _resident((1, D)), _resident((1, D))],
        out_specs=[tile(D), tile(D)],
        out_shape=[jax.ShapeDtypeStruct((B, S, D), F32), jax.ShapeDtypeStruct((B, S, D), BF16)],
        compiler_params=_params(("parallel", "parallel")),
        name="mix_out_ln",
    )(x, attn, rec_n, w_out, row(gn_attn, ATTN_WIDTH), row(ln_g, D), row(ln_b, D))


ROUTER_TILE = 1024
N_CAND_ROWS = 16 + 7 * 8 + 8
ROUTER_HEADS_PER_TRIP = 1


def _sort_network(n):
    def merge(lo, hi, r):
        step = r * 2
        if step < hi - lo:
            yield from merge(lo, hi, step)
            yield from merge(lo + r, hi, step)
            yield from ((i, i + r) for i in range(lo + r, hi - r, step))
        else:
            yield (lo, lo + r)

    def sort(lo, hi):
        if hi - lo >= 1:
            mid = lo + (hi - lo) // 2
            yield from sort(lo, mid)
            yield from sort(mid + 1, hi)
            yield from merge(lo, hi, 1)

    return tuple(sort(0, n - 1))


_SORT16 = _sort_network(PEER_TOPK)
_BITONIC16 = tuple((i, i + d) for d in (8, 4, 2, 1) for i in range(PEER_TOPK) if i & d == 0)


def _exchange(x, pairs):
    for i, j in pairs:
        a, b = x[i], x[j]
        if a is None:
            x[i], x[j] = b, None
        elif b is not None:
            x[i], x[j] = jnp.maximum(a, b), jnp.minimum(a, b)
    return x


def _top16_sorted(blocks):
    x = _exchange(list(blocks) + [None] * (PEER_TOPK - len(blocks)), _SORT16)
    for shift in (4, 2, 1):
        partner = [None if v is None else pltpu.roll(v, shift, 0) for v in x]
        merged = []
        for i in range(PEER_TOPK):
            a, b = x[i], partner[PEER_TOPK - 1 - i]
            merged.append(b if a is None else a if b is None else jnp.maximum(a, b))
        x = _exchange(merged, _BITONIC16)
    return x


def _sublane_blocks(a):
    return [a[i * SUBLANES:(i + 1) * SUBLANES] for i in range(a.shape[0] // SUBLANES)]


def _count(flags):
    total = sum(jnp.where(f, 1.0, 0.0) for f in flags)
    return jnp.sum(total, axis=0, keepdims=True)


def _route_head(s1, s2):
    b1, b2 = _sublane_blocks(s1), _sublane_blocks(s2)
    v1, v2 = _top16_sorted(b1), _top16_sorted(b2)
    rows = (v1[0].shape[0], v1[0].shape[1])
    sub = lax.broadcasted_iota(jnp.int32, rows, 0)

    def on_sublanes(vals):
        out = vals[0]
        for j in range(1, SUBLANES):
            out = jnp.where(sub == j, vals[j], out)
        return out

    v2_lo, v2_hi, v1_hi = on_sublanes(v2[:8]), on_sublanes(v2[8:]), on_sublanes(v1[8:])
    cand = [v1[0] + v2_lo, v1[0] + v2_hi] + [v1[r] + v2_lo for r in range(1, 8)] + [v1_hi + v2[0]]
    best = _top16_sorted(cand)
    chosen = [c >= best[PEER_TOPK - 1] for c in cand]
    picked = [jnp.where(f, 1.0, 0.0) for f in chosen]
    z = jnp.sum(sum(p * jnp.exp(c - best[0]) for p, c in zip(picked, cand)), axis=0, keepdims=True)
    cnt_r = [jnp.sum(picked[0] + picked[1], axis=0, keepdims=True)]
    cnt_r += [jnp.sum(picked[1 + r], axis=0, keepdims=True) for r in range(1, 8)]
    cnt_r += [picked[9][j:j + 1] for j in range(SUBLANES)]

    cnt_r = [jnp.broadcast_to(c, rows) for c in cnt_r]
    rank2, cnt_k, coef, e2 = [], [], [], []
    inv_z = 1.0 / z
    for blk1, blk2 in zip(b1, b2):
        r2 = jnp.full(rows, float(PEER_TOPK), F32)
        ck = jnp.zeros(rows, F32)
        for r in range(PEER_TOPK - 1, -1, -1):
            r2 = jnp.where(blk2 >= v2[r], float(r), r2)
            ck = jnp.where(blk1 == v1[r], cnt_r[r], ck)
        rank2.append(r2)
        cnt_k.append(ck)
        coef.append(jnp.exp(blk1 - v1[0]) * inv_z)
        e2.append(jnp.exp(blk2 - v2[0]))

    strict = [v[r] > v[r + 1] for v in (v1, v2) for r in range(PEER_TOPK - 1)]
    ok = _count(strict) == float(SUBLANES * len(strict))
    for blocks, vals in ((b1, v1), (b2, v2), (cand, best)):
        ok = ok & (_count([b >= vals[PEER_TOPK - 1] for b in blocks]) == float(PEER_TOPK))
    redo = jnp.max(jnp.where(ok, 0.0, 1.0))
    cat = lambda parts: jnp.concatenate(parts, axis=0)
    return cat(rank2), cat(e2), cat(coef), cat(cnt_k), redo


def _top16_rows_any(s):
    idx = lax.broadcasted_iota(jnp.int32, s.shape, 0)
    rank = jnp.full(s.shape, float(PEER_TOPK), F32)
    vals = []
    for r in range(PEER_TOPK):
        m = jnp.max(s, axis=0, keepdims=True)
        sel = idx == jnp.min(jnp.where(s == m, idx, s.shape[0]), axis=0, keepdims=True)
        rank = jnp.where(sel, float(r), rank)
        s = jnp.where(sel, -jnp.inf, s)
        vals.append(m)
    return rank, jnp.concatenate(vals, axis=0)


def _route_head_any(s1, s2):
    rank1, v1 = _top16_rows_any(s1)
    rank2, v2 = _top16_rows_any(s2)
    blocks = [v1[0:1] + v2]
    for r1 in range(1, 8):
        blocks.append(v1[r1:r1 + 1] + v2[0:8])
    blocks.append(v1[8:16] + v2[0:1])
    cand = jnp.concatenate(blocks, axis=0)
    rank3, _ = _top16_rows_any(cand)
    chosen = jnp.where(rank3 < PEER_TOPK, 1.0, 0.0)
    z = jnp.sum(chosen * jnp.exp(cand - cand[0:1]), axis=0, keepdims=True)
    counts = [jnp.sum(chosen[0:16], axis=0, keepdims=True)]
    for r1 in range(1, 8):
        counts.append(jnp.sum(chosen[8 + 8 * r1: 16 + 8 * r1], axis=0, keepdims=True))
    counts.append(chosen[N_CAND_ROWS - 8:])
    cnt_r = jnp.concatenate(counts, axis=0)
    cnt_k = jnp.zeros(s1.shape, F32)
    for r in range(PEER_TOPK):
        cnt_k = jnp.where(rank1 == r, cnt_r[r:r + 1], cnt_k)
    return rank2, jnp.exp(s2 - v2[0:1]), jnp.exp(s1 - v1[0:1]) / z, cnt_k


def _router_kernel(h_ref, wq_ref, k1_ref, k2_ref, rank2_ref, e2_ref, coef_ref, cnt_ref):
    hb = h_ref[...]

    def emit(h, rank2, e2, coef, cnt_k):
        rank2_ref[h] = rank2.astype(BF16)
        e2_ref[h] = e2.astype(BF16)
        coef_ref[h] = coef
        cnt_ref[h] = cnt_k

    def heads(g, carry):
        todo = []
        for i in range(ROUTER_HEADS_PER_TRIP):
            h = g * ROUTER_HEADS_PER_TRIP + i
            q_t = lax.dot_general(wq_ref[h], hb, (((1,), (1,)), ((), ())), preferred_element_type=F32)
            s1 = jnp.dot(k1_ref[...], q_t[:PEER_HALF].astype(BF16), preferred_element_type=F32)
            s2 = jnp.dot(k2_ref[...], q_t[PEER_HALF:].astype(BF16), preferred_element_type=F32)
            *stats, redo = _route_head(s1, s2)
            emit(h, *stats)
            todo.append((h, s1, s2, redo))
        for h, s1, s2, redo in todo:
            pl.when(redo > 0.0)(functools.partial(lambda h, s1, s2: emit(h, *_route_head_any(s1, s2)), h, s1, s2))
        return carry

    lax.fori_loop(0, PEER_HEADS // ROUTER_HEADS_PER_TRIP, heads, 0)


def _router_call(hb, wq_t, k1, k2):
    T, D = hb.shape
    tm = ROUTER_TILE
    o_spec = pl.BlockSpec((PEER_HEADS, PEER_NKEYS, tm), lambda t: (0, 0, t))
    shape = lambda dt: jax.ShapeDtypeStruct((PEER_HEADS, PEER_NKEYS, T), dt)
    return pl.pallas_call(
        _router_kernel,
        grid=(T // tm,),
        in_specs=[pl.BlockSpec((tm, D), lambda t: (t, 0)),
                  _resident(wq_t.shape), _resident(k1.shape), _resident(k2.shape)],
        out_specs=[o_spec] * 4,
        out_shape=[shape(BF16), shape(BF16), shape(F32), shape(F32)],
        compiler_params=_params(("parallel",)),
        name="peer_router",
    )(hb, wq_t, k1, k2)


PREP_TILE = 512


def _uvprep_kernel(u_ref, v_ref, ub_ref, vt_ref):
    ub_ref[...] = u_ref[...].astype(BF16)
    vt_ref[...] = v_ref[...].T.astype(BF16)


def _uvprep_call(u, v):
    E, D = u.shape
    te = PREP_TILE
    return pl.pallas_call(
        _uvprep_kernel,
        grid=(E // te,),
        in_specs=[pl.BlockSpec((te, D), lambda e: (e, 0)), pl.BlockSpec((te, D), lambda e: (e, 0))],
        out_specs=[pl.BlockSpec((te, D), lambda e: (e, 0)), pl.BlockSpec((D, te), lambda e: (0, e))],
        out_shape=[jax.ShapeDtypeStruct((E, D), BF16), jax.ShapeDtypeStruct((D, E), BF16)],
        compiler_params=_params(("parallel",)),
        name="peer_uvprep",
    )(u, v)


PEER_TOK_TILE = 512
PEER_EXP_TILE = 1024
KEYS_PER_STEP = PEER_EXP_TILE // PEER_NKEYS
PEER_STAGES = 3


def _peer_kernel(h_ref, u_ref, vt_ref, rank2_ref, e2_ref, coef_ref, cnt_ref, o_ref, act_ref, p_ref, *, n_exp_tiles):
    s = pl.program_id(0)
    cur = s % 2
    prev = (s + 1) % 2

    @pl.when(s == 0)
    def _():
        act_ref[1] = jnp.zeros(act_ref.shape[1:], BF16)
        p_ref[0] = jnp.zeros(p_ref.shape[1:], BF16)

    @pl.when(jnp.maximum(s - (PEER_STAGES - 1), 0) % n_exp_tiles == 0)
    def _():
        o_ref[...] = jnp.zeros(o_ref.shape, F32)

    act_ref[cur] = lax.dot_general(u_ref[...], h_ref[...], (((1,), (1,)), ((), ())),
                                   preferred_element_type=F32).astype(BF16)

    for key in range(KEYS_PER_STEP):
        gate = None
        for h in range(PEER_HEADS):
            cnt = cnt_ref[h, key:key + 1, :].astype(BF16)
            cf = coef_ref[h, key:key + 1, :].astype(BF16)
            term = jnp.where(rank2_ref[h] < cnt, e2_ref[h], jnp.zeros((), BF16)) * cf
            gate = term if gate is None else gate + term
        rows = slice(key * PEER_NKEYS, (key + 1) * PEER_NKEYS)
        p_ref[prev, rows, :] = gate * _gelu_tanh(act_ref[prev, rows, :])

    o_ref[...] += jnp.dot(vt_ref[...], p_ref[cur], preferred_element_type=F32)


def _peer_call(hb, ub, vt, rank2, e2, coef, cnt):
    T, D = hb.shape
    tm, te = PEER_TOK_TILE, PEER_EXP_TILE
    n_e = PEER_EXPERTS // te
    n_pairs = (T // tm) * n_e
    by_tile = lambda a: a.reshape(PEER_HEADS, n_e, KEYS_PER_STEP, T)

    def pair(s, lag):
        i = jnp.clip(s - lag, 0, n_pairs - 1)
        return i // n_e, i % n_e

    head_spec = pl.BlockSpec((PEER_HEADS, PEER_NKEYS, tm), lambda s: (0, 0, pair(s, 1)[0]))
    key_spec = pl.BlockSpec((PEER_HEADS, None, KEYS_PER_STEP, tm),
                            lambda s: (0, pair(s, 1)[1], 0, pair(s, 1)[0]))
    return pl.pallas_call(
        functools.partial(_peer_kernel, n_exp_tiles=n_e),
        grid=(n_pairs + PEER_STAGES - 1,),
        in_specs=[pl.BlockSpec((tm, D), lambda s: (pair(s, 0)[0], 0)),
                  pl.BlockSpec((te, D), lambda s: (pair(s, 0)[1], 0)),
                  pl.BlockSpec((D, te), lambda s: (0, pair(s, 2)[1])),
                  head_spec, head_spec, key_spec, key_spec],
        out_specs=pl.BlockSpec((D, tm), lambda s: (0, pair(s, 2)[0])),
        out_shape=jax.ShapeDtypeStruct((D, T), F32),
        scratch_shapes=[pltpu.VMEM((2, te, tm), BF16), pltpu.VMEM((2, te, tm), BF16)],
        compiler_params=_params(("arbitrary",)),
        name="peer_dense",
    )(hb, ub, vt, rank2, e2, by_tile(coef), by_tile(cnt))


FINAL_TILE = 512


def _final_kernel(h_ref, yt_ref, g_ref, b_ref, o_ref, *, depth):
    o_ref[...] = _layer_norm(_deepnorm_alpha(depth) * h_ref[...] + yt_ref[...].T, g_ref[...], b_ref[...])


def _final_call(h1, y_t, ln_g, ln_b, depth):
    T, D = h1.shape
    tm = FINAL_TILE
    return pl.pallas_call(
        functools.partial(_final_kernel, depth=depth),
        grid=(T // tm,),
        in_specs=[pl.BlockSpec((tm, D), lambda t: (t, 0)),
                  pl.BlockSpec((D, tm), lambda t: (0, t)),
                  _resident((1, D)), _resident((1, D))],
        out_specs=pl.BlockSpec((tm, D), lambda t: (t, 0)),
        out_shape=jax.ShapeDtypeStruct((T, D), F32),
        compiler_params=_params(("parallel",)),
        name="final_ln",
    )(h1, y_t, ln_g.reshape(1, D), ln_b.reshape(1, D))


def _mixer(h, positions, w_in, conv_w, conv_b, rg_w_a, rg_b_a, rg_w_x, rg_b_x, lru_lambda,
           gn_attn, gn_rec, w_out, ln1_g, ln1_b, peer_wq, peer_u, peer_v, depth):
    B, S, D = h.shape
    half = HEAD_DIM // 2
    inv = ROPE_THETA ** (-jnp.arange(half, dtype=F32) / half)
    inv_full = jnp.concatenate([inv, inv]).reshape(1, HEAD_DIM)
    w_qkv = w_in[:, :3 * ATTN_WIDTH].astype(BF16)
    w_rec = w_in[:, 3 * ATTN_WIDTH:].astype(BF16)
    wax = jnp.concatenate([rg_w_a, rg_w_x], axis=-1).astype(BF16)
    q, k, v = _qkv_call(h, positions.reshape(B, S, 1), inv_full, w_qkv)
    rec_n = _rec_call(h, w_rec, conv_w, conv_b, wax, rg_b_a, rg_b_x, lru_lambda, gn_rec)
    n_steps = B * ATTN_HEADS
    wq_rider = None
    if PEER_HEADS == ATTN_HEADS and D % B == 0:
        wq_rider = _Rider(peer_wq, (D // B, 2 * PEER_HALF), lambda b, h: (b, h),
                          (PEER_HEADS, 2 * PEER_HALF, D), (None, 2 * PEER_HALF, D // B), lambda b, h: (h, 0, b), True)
    wanted = {"w_out": _row_slices(w_out, n_steps, ATTN_HEADS, False), "wq_t": wq_rider,
              "ub": _row_slices(peer_u, n_steps, ATTN_HEADS, False),
              "vt": _row_slices(peer_v, n_steps, ATTN_HEADS, True)}
    names = [n for n, r in wanted.items() if r is not None and r.fits]
    attn, *converted = _attn_call(q, k, v, [wanted[n] for n in names])
    done = dict(zip(names, converted))
    w_out_b = done["w_out"] if "w_out" in done else w_out.astype(BF16)
    h1, h1b = _mix_call(h, attn, rec_n, w_out_b, gn_attn, ln1_g, ln1_b, depth)
    return h1, h1b, done


def _peer(h, hb, wq, keys1, keys2, u, v, done, ln2_g, ln2_b, depth):
    B, S, D = h.shape
    h2 = h.reshape(B * S, D)
    hb2 = hb.reshape(B * S, D)
    if "wq_t" in done:
        wq_t = done["wq_t"]
    else:
        wq_t = wq.reshape(D, PEER_HEADS, 2 * PEER_HALF).transpose(1, 2, 0).astype(BF16)
    rank2, e2, coef, cnt = _router_call(hb2, wq_t, keys1.astype(BF16), keys2.astype(BF16))
    if "ub" in done and "vt" in done:
        ub, vt = done["ub"], done["vt"]
    else:
        ub, vt = _uvprep_call(u, v)
    y_t = _peer_call(hb2, ub, vt, rank2, e2, coef, cnt)
    return _final_call(h2, y_t, ln2_g, ln2_b, depth).reshape(B, S, D)


def kernel(x, positions, w_in, conv_w, conv_b, rg_w_a, rg_b_a, rg_w_x, rg_b_x, lru_lambda, gn_attn, gn_rec,
           w_out, ln1_g, ln1_b, peer_wq, peer_keys1, peer_keys2, peer_u, peer_v, ln2_g, ln2_b):
    depth = w_in.shape[0]
    h = x
    for l in range(depth):
        h, hb, done = _mixer(h, positions, w_in[l], conv_w[l], conv_b[l], rg_w_a[l], rg_b_a[l], rg_w_x[l],
                             rg_b_x[l], lru_lambda[l], gn_attn[l], gn_rec[l], w_out[l], ln1_g[l], ln1_b[l],
                             peer_wq[l], peer_u[l], peer_v[l], depth)
        h = _peer(h, hb, peer_wq[l], peer_keys1[l], peer_keys2[l], peer_u[l], peer_v[l], done, ln2_g[l],
                  ln2_b[l], depth)
    return h
```

```python
import functools
import math

import jax
import jax.numpy as jnp
import numpy as np
from jax import lax
from jax.experimental import pallas as pl
from jax.experimental.pallas import tpu as pltpu

F32 = jnp.float32
BF16 = jnp.bfloat16

ATTN_HEADS = 8
HEAD_DIM = 128
ATTN_WIDTH = ATTN_HEADS * HEAD_DIM
REC_BLOCKS = 8
REC_BLOCK = 128
REC_WIDTH = REC_BLOCKS * REC_BLOCK
CONV_WIDTH = 4
LRU_C = 8.0
BAND = 128
DILATIONS = (1, 4, 16)
ROPE_THETA = 10000.0
NEG_INF = -1e30
PEER_HEADS = 8
PEER_NKEYS = 128
PEER_EXPERTS = PEER_NKEYS * PEER_NKEYS
PEER_HALF = 128
PEER_TOPK = 16
SUBLANES = 8

V7X_VMEM_LIMIT = 56 * 1024 * 1024


def _params(sem, vmem=V7X_VMEM_LIMIT):
    return pltpu.CompilerParams(dimension_semantics=sem, vmem_limit_bytes=vmem)


def _resident(shape):
    zeros = (0,) * len(shape)
    return pl.BlockSpec(shape, lambda *_: zeros, pipeline_mode=pl.Buffered(1))


def _exact(v):
    assert float(np.float32(v)) == float(v)
    return jnp.full((1, 1), v, F32)


_PI_HI = float(np.float32(math.pi))
_PI_MID = float(np.float32(math.pi - _PI_HI))
_PI_LO = float(np.float32(math.pi - _PI_HI - _PI_MID))


def _ln_eps():
    return _exact(1.0) / _exact(1e5)


def _gelu_tanh(x):
    pi = _exact(_PI_HI) + _exact(_PI_MID) + _exact(_PI_LO)
    c = jnp.sqrt(_exact(2.0) / pi).astype(x.dtype)
    k = (_exact(44715.0) / _exact(1e6)).astype(x.dtype)
    return 0.5 * x * (1.0 + jnp.tanh(c * (x + k * (x * x * x))))


def _sigmoid(x):
    return 1.0 / (1.0 + jnp.exp(-x))


def _rms_scale(x):
    return lax.rsqrt(jnp.mean(x * x, axis=-1, keepdims=True) + _ln_eps())


def _layer_norm(z, g, b):
    mu = jnp.mean(z, axis=-1, keepdims=True)
    zc = z - mu
    var = jnp.mean(zc * zc, axis=-1, keepdims=True)
    return zc * lax.rsqrt(var + _ln_eps()) * g + b


def _deepnorm_alpha(depth):
    return jnp.sqrt(jnp.sqrt(_exact(2.0 * depth)))


QKV_TILE = 512


def _qkv_kernel(x_ref, pos_ref, inv_ref, w_ref, *rest, n_riders):
    srcs, (q_ref, k_ref, v_ref), dsts = rest[:n_riders], rest[n_riders:n_riders + 3], rest[n_riders + 3:]
    for src, dst in zip(srcs, dsts):
        dst[...] = src[...].astype(BF16)
    proj = jnp.dot(x_ref[0].astype(BF16), w_ref[...], preferred_element_type=F32)
    ang = pos_ref[0].astype(F32) * inv_ref[...]
    lane = lax.broadcasted_iota(jnp.int32, ang.shape, 1)
    cos = jnp.cos(ang)
    sin = jnp.where(lane < HEAD_DIM // 2, -jnp.sin(ang), jnp.sin(ang))
    scale = lax.rsqrt(_exact(float(HEAD_DIM)))
    for h in range(ATTN_HEADS):
        cols = slice(h * HEAD_DIM, (h + 1) * HEAD_DIM)
        qh = proj[:, cols]
        kh = proj[:, ATTN_WIDTH + h * HEAD_DIM: ATTN_WIDTH + (h + 1) * HEAD_DIM]
        q_ref[0, :, cols] = (qh * cos + pltpu.roll(qh, HEAD_DIM // 2, 1) * sin) * scale
        k_ref[0, :, cols] = kh * cos + pltpu.roll(kh, HEAD_DIM // 2, 1) * sin
    v_ref[0] = proj[:, 2 * ATTN_WIDTH:]


def _qkv_call(x, pos, inv_full, w_qkv, riders=()):
    B, S, D = x.shape
    tm = QKV_TILE
    spec = pl.BlockSpec((1, tm, ATTN_WIDTH), lambda b, s: (b, s, 0))
    shape = jax.ShapeDtypeStruct((B, S, ATTN_WIDTH), F32)
    assert not any(r.transposed for r in riders)
    return pl.pallas_call(
        functools.partial(_qkv_kernel, n_riders=len(riders)),
        grid=(B, S // tm),
        in_specs=[pl.BlockSpec((1, tm, D), lambda b, s: (b, s, 0)),
                  pl.BlockSpec((1, tm, 1), lambda b, s: (b, s, 0)),
                  _resident((1, HEAD_DIM)),
                  _resident(w_qkv.shape)] + [r.in_spec for r in riders],
        out_specs=[spec, spec, spec] + [r.out_spec for r in riders],
        out_shape=[shape, shape, shape] + [r.out_shape for r in riders],
        compiler_params=_params(("parallel", "parallel")),
        name="qkv_rope",
    )(x, pos, inv_full, w_qkv, *[r.src for r in riders])


REC_TILE = 512


def _rec_kernel(x_ref, wx_ref, wg_ref, cw_ref, cb_ref, wax_ref, ba_ref, bx_ref, lam_ref, g_ref, o_ref,
                ext_ref, a_ref, b_ref, h_ref, hc_ref):
    tm = REC_TILE

    @pl.when(pl.program_id(1) == 0)
    def _():
        ext_ref[0:SUBLANES, :] = jnp.zeros((SUBLANES, REC_WIDTH), F32)
        hc_ref[...] = jnp.zeros((SUBLANES, REC_WIDTH), F32)

    xb = x_ref[0].astype(BF16)
    xr = jnp.dot(xb, wx_ref[...], preferred_element_type=F32)
    gr = jnp.dot(xb, wg_ref[...], preferred_element_type=F32)
    ext_ref[SUBLANES:2 * SUBLANES, :] = xr[:SUBLANES, :]
    xc = cb_ref[...] + cw_ref[CONV_WIDTH - 1:CONV_WIDTH, :] * xr
    xc_head = xc[:SUBLANES, :]
    for j in range(CONV_WIDTH - 1):
        back = CONV_WIDTH - 1 - j
        xc = xc + cw_ref[j:j + 1, :] * pltpu.roll(xr, back, 0)
        xc_head = xc_head + cw_ref[j:j + 1, :] * ext_ref[pl.ds(SUBLANES - back, SUBLANES), :]
    xc = jnp.concatenate([xc_head, xc[SUBLANES:, :]], axis=0)
    ext_ref[0:SUBLANES, :] = xr[tm - SUBLANES:, :]

    rs, is_ = [], []
    for n in range(REC_BLOCKS):
        xb = xc[:, n * REC_BLOCK:(n + 1) * REC_BLOCK].astype(BF16)
        gts = jnp.dot(xb, wax_ref[n], preferred_element_type=F32)
        rs.append(gts[:, :REC_BLOCK])
        is_.append(gts[:, REC_BLOCK:])
    r = _sigmoid(jnp.concatenate(rs, axis=1) + ba_ref[...])
    ig = _sigmoid(jnp.concatenate(is_, axis=1) + bx_ref[...])
    z = -lam_ref[...]
    softplus = jnp.maximum(z, 0.0) + jnp.log(1.0 + jnp.exp(-jnp.abs(z)))
    log_a = (-LRU_C) * r * softplus
    a = jnp.exp(log_a)
    gap = 1.0 - jnp.exp(2.0 * log_a)
    bt = jnp.where(gap > 0.0, gap * lax.rsqrt(gap), 0.0) * (ig * xc)

    r8 = lax.broadcasted_iota(jnp.int32, a.shape, 0) & (SUBLANES - 1)
    for s in (1, 2, 4):
        a_sh = pltpu.roll(a, s, 0)
        b_sh = pltpu.roll(bt, s, 0)
        m = r8 >= s
        bt = jnp.where(m, a * b_sh + bt, bt)
        a = jnp.where(m, a * a_sh, a)
    a_ref[...] = a
    b_ref[...] = bt

    def group(g, hc):
        rows = pl.ds(pl.multiple_of(g * SUBLANES, SUBLANES), SUBLANES)
        h = a_ref[rows, :] * hc + b_ref[rows, :]
        h_ref[rows, :] = h
        return jnp.broadcast_to(h[SUBLANES - 1:SUBLANES, :], (SUBLANES, REC_WIDTH))

    hc_ref[...] = lax.fori_loop(0, tm // SUBLANES, group, hc_ref[...], unroll=8)

    rec = h_ref[...] * _gelu_tanh(gr)
    o_ref[0] = (rec * _rms_scale(rec) * g_ref[...]).astype(BF16)


def _rec_call(x, w_x, w_g, conv_w, conv_b, wax, b_a, b_x, lam, gn_rec):
    B, S, D = x.shape
    tm = REC_TILE
    row = lambda a: a.reshape(1, REC_WIDTH)
    return pl.pallas_call(
        _rec_kernel,
        grid=(B, S // tm),
        in_specs=[pl.BlockSpec((1, tm, D), lambda b, s: (b, s, 0)),
                  _resident(w_x.shape), _resident(w_g.shape),
                  _resident((CONV_WIDTH, REC_WIDTH)),
                  _resident((1, REC_WIDTH)),
                  _resident(wax.shape),
                  _resident((1, REC_WIDTH)), _resident((1, REC_WIDTH)),
                  _resident((1, REC_WIDTH)), _resident((1, REC_WIDTH))],
        out_specs=pl.BlockSpec((1, tm, REC_WIDTH), lambda b, s: (b, s, 0)),
        out_shape=jax.ShapeDtypeStruct((B, S, REC_WIDTH), BF16),
        scratch_shapes=[pltpu.VMEM((2 * SUBLANES, REC_WIDTH), F32),
                        pltpu.VMEM((tm, REC_WIDTH), F32),
                        pltpu.VMEM((tm, REC_WIDTH), F32),
                        pltpu.VMEM((tm, REC_WIDTH), F32),
                        pltpu.VMEM((SUBLANES, REC_WIDTH), F32)],
        compiler_params=_params(("parallel", "arbitrary")),
        name="rec_branch",
    )(x, w_x, w_g, conv_w, row(conv_b), wax, row(b_a), row(b_x), row(lam), row(gn_rec))


ATTN_UNROLL = 16


def _band_block(q, k, v, valid):
    s = lax.dot_general(q.astype(BF16), k.astype(BF16), (((1,), (1,)), ((), ())), preferred_element_type=F32)
    s = jnp.where(valid, s, NEG_INF)
    m = jnp.max(s, axis=-1, keepdims=True)
    p = jnp.exp(s - m)
    l = jnp.sum(p, axis=-1, keepdims=True)
    acc = jnp.dot(p.astype(BF16), v.astype(BF16), preferred_element_type=F32)
    return acc, m, l


def _attn_kernel(q_ref, k_ref, v_ref, *rest, seq, transposed):
    n = len(transposed)
    srcs, o_ref, dsts, stats = rest[:n], rest[n], rest[n + 1:2 * n + 1], rest[2 * n + 1:]
    for src, dst, flip in zip(srcs, dsts, transposed):
        dst[...] = (src[...].T if flip else src[...]).astype(BF16)
    wide = (BAND, HEAD_DIM)

    for p, dil in enumerate(DILATIONS):
        acc_ref, m_ref, l_ref = stats[3 * p: 3 * p + 3]
        n_blk = seq // (dil * BAND)

        def block(idx, carry, dil=dil, n_blk=n_blk, acc_ref=acc_ref, m_ref=m_ref, l_ref=l_ref):
            c = idx // n_blk
            j = idx - c * n_blk
            k0 = jnp.maximum(j - 1, 0) * BAND
            n_keys = 2 * BAND if n_blk > 1 else BAND
            if dil == 1:
                rows = pl.ds(pl.multiple_of(j * BAND, BAND), BAND)
                krows = pl.ds(pl.multiple_of(k0, BAND), n_keys)
            else:
                rows = pl.ds(dil * j * BAND + c, BAND, stride=dil)
                krows = pl.ds(dil * k0 + c, n_keys, stride=dil)
            qi = lax.broadcasted_iota(jnp.int32, (BAND, n_keys), 0)
            kj = lax.broadcasted_iota(jnp.int32, (BAND, n_keys), 1)
            dist = (j * BAND + qi) - (k0 + kj)
            acc, m, l = _band_block(q_ref[0, rows, :], k_ref[0, krows, :], v_ref[0, krows, :],
                                    (dist >= 0) & (dist <= BAND))
            acc_ref[rows, :] = acc
            m_ref[rows, :] = jnp.broadcast_to(m, wide)
            l_ref[rows, :] = jnp.broadcast_to(l, wide)
            return carry

        lax.fori_loop(0, seq // BAND, block, 0, unroll=ATTN_UNROLL)

    def merge(j, carry):
        rows = pl.ds(pl.multiple_of(j * BAND, BAND), BAND)
        ms = [stats[3 * p + 1][rows, :] for p in range(3)]
        mx = jnp.maximum(jnp.maximum(ms[0], ms[1]), ms[2])
        ws = [jnp.exp(mm - mx) for mm in ms]
        num = sum(ws[p] * stats[3 * p][rows, :] for p in range(3))
        den = sum(ws[p] * stats[3 * p + 2][rows, :] for p in range(3))
        o_ref[0, rows, :] = num / den
        return carry

    lax.fori_loop(0, seq // BAND, merge, 0, unroll=ATTN_UNROLL)


ATTN_RIDER_BYTES_MAX = 4 * 1024 * 1024


class _Rider:
    def __init__(self, src, in_block, in_index, out_shape, out_block, out_index, transposed):
        self.src, self.transposed = src, transposed
        self.in_spec = pl.BlockSpec(in_block, in_index)
        self.out_spec = pl.BlockSpec(out_block, out_index)
        self.out_shape = jax.ShapeDtypeStruct(out_shape, BF16)
        rows, cols = [d for d in in_block if d is not None]
        self.fits = rows % 16 == 0 and cols % HEAD_DIM == 0 and rows * cols * 4 <= ATTN_RIDER_BYTES_MAX
        if transposed:
            self.fits = self.fits and rows % HEAD_DIM == 0


def _row_slices(w, n_steps, n_heads, transposed):
    rows, cols = w.shape
    r = rows // n_steps
    step = lambda b, h: b * n_heads + h
    if r * n_steps != rows:
        return None
    if transposed:
        return _Rider(w, (r, cols), lambda b, h: (step(b, h), 0), (cols, rows), (cols, r),
                      lambda b, h: (0, step(b, h)), True)
    return _Rider(w, (r, cols), lambda b, h: (step(b, h), 0), (rows, cols), (r, cols),
                  lambda b, h: (step(b, h), 0), False)


def _attn_call(q, k, v, riders=()):
    B, S, _ = q.shape
    assert S % (max(DILATIONS) * BAND) == 0
    spec = pl.BlockSpec((1, S, HEAD_DIM), lambda b, h: (b, 0, h))
    stat = pltpu.VMEM((S, HEAD_DIM), F32)
    outs = pl.pallas_call(
        functools.partial(_attn_kernel, seq=S, transposed=tuple(r.transposed for r in riders)),
        grid=(B, ATTN_HEADS),
        in_specs=[spec] * 3 + [r.in_spec for r in riders],
        out_specs=[spec] + [r.out_spec for r in riders],
        out_shape=[jax.ShapeDtypeStruct((B, S, ATTN_WIDTH), F32)] + [r.out_shape for r in riders],
        scratch_shapes=[stat] * 9,
        compiler_params=_params(("parallel", "parallel")),
        name="dilated_attn",
    )(q, k, v, *[r.src for r in riders])
    return tuple(outs)


MIX_TILE = 512


def _mix_kernel(x_ref, at_ref, rc_ref, wo_ref, ga_ref, g_ref, b_ref, o_ref, ob_ref, *, depth):
    at = at_ref[0]
    at_n = (at * _rms_scale(at) * ga_ref[...]).astype(BF16)
    y = jnp.dot(at_n, wo_ref[:ATTN_WIDTH, :], preferred_element_type=F32)
    y = y + jnp.dot(rc_ref[0], wo_ref[ATTN_WIDTH:, :], preferred_element_type=F32)
    h = _layer_norm(_deepnorm_alpha(depth) * x_ref[0] + y, g_ref[...], b_ref[...])
    o_ref[0] = h
    ob_ref[0] = h.astype(BF16)


def _mix_call(x, attn, rec_n, w_out, gn_attn, ln_g, ln_b, depth):
    B, S, D = x.shape
    tm = MIX_TILE
    row = lambda a, n: a.reshape(1, n)
    tile = lambda n: pl.BlockSpec((1, tm, n), lambda b, s: (b, s, 0))
    return pl.pallas_call(
        functools.partial(_mix_kernel, depth=depth),
        grid=(B, S // tm),
        in_specs=[tile(D), tile(ATTN_WIDTH), tile(REC_WIDTH),
                  _resident(w_out.shape),
                  _resident((1, ATTN_WIDTH)), _resident((1, D)), _resident((1, D))],
        out_specs=[tile(D), tile(D)],
        out_shape=[jax.ShapeDtypeStruct((B, S, D), F32), jax.ShapeDtypeStruct((B, S, D), BF16)],
        compiler_params=_params(("parallel", "parallel")),
        name="mix_out_ln",
    )(x, attn, rec_n, w_out, row(gn_attn, ATTN_WIDTH), row(ln_g, D), row(ln_b, D))


ROUTER_TILE = 1024
N_CAND_ROWS = 16 + 7 * 8 + 8
ROUTER_HEADS_PER_TRIP = 1


def _sort_network(n):
    def merge(lo, hi, r):
        step = r * 2
        if step < hi - lo:
            yield from merge(lo, hi, step)
            yield from merge(lo + r, hi, step)
            yield from ((i, i + r) for i in range(lo + r, hi - r, step))
        else:
            yield (lo, lo + r)

    def sort(lo, hi):
        if hi - lo >= 1:
            mid = lo + (hi - lo) // 2
            yield from sort(lo, mid)
            yield from sort(mid + 1, hi)
            yield from merge(lo, hi, 1)

    return tuple(sort(0, n - 1))


_SORT16 = _sort_network(PEER_TOPK)
_BITONIC16 = tuple((i, i + d) for d in (8, 4, 2, 1) for i in range(PEER_TOPK) if i & d == 0)


def _exchange(x, pairs):
    for i, j in pairs:
        a, b = x[i], x[j]
        if a is None:
            x[i], x[j] = b, None
        elif b is not None:
            x[i], x[j] = jnp.maximum(a, b), jnp.minimum(a, b)
    return x


def _top16_sorted(blocks):
    x = _exchange(list(blocks) + [None] * (PEER_TOPK - len(blocks)), _SORT16)
    for shift in (4, 2, 1):
        partner = [None if v is None else pltpu.roll(v, shift, 0) for v in x]
        merged = []
        for i in range(PEER_TOPK):
            a, b = x[i], partner[PEER_TOPK - 1 - i]
            merged.append(b if a is None else a if b is None else jnp.maximum(a, b))
        x = _exchange(merged, _BITONIC16)
    return x


def _sublane_blocks(a):
    return [a[i * SUBLANES:(i + 1) * SUBLANES] for i in range(a.shape[0] // SUBLANES)]


def _count(flags):
    total = sum(jnp.where(f, 1.0, 0.0) for f in flags)
    return jnp.sum(total, axis=0, keepdims=True)


def _route_head(s1, s2):
    b1, b2 = _sublane_blocks(s1), _sublane_blocks(s2)
    v1, v2 = _top16_sorted(b1), _top16_sorted(b2)
    rows = (v1[0].shape[0], v1[0].shape[1])
    sub = lax.broadcasted_iota(jnp.int32, rows, 0)

    def on_sublanes(vals):
        out = vals[0]
        for j in range(1, SUBLANES):
            out = jnp.where(sub == j, vals[j], out)
        return out

    v2_lo, v2_hi, v1_hi = on_sublanes(v2[:8]), on_sublanes(v2[8:]), on_sublanes(v1[8:])
    cand = [v1[0] + v2_lo, v1[0] + v2_hi] + [v1[r] + v2_lo for r in range(1, 8)] + [v1_hi + v2[0]]
    best = _top16_sorted(cand)
    chosen = [c >= best[PEER_TOPK - 1] for c in cand]
    picked = [jnp.where(f, 1.0, 0.0) for f in chosen]
    z = jnp.sum(sum(p * jnp.exp(c - best[0]) for p, c in zip(picked, cand)), axis=0, keepdims=True)
    cnt_r = [jnp.sum(picked[0] + picked[1], axis=0, keepdims=True)]
    cnt_r += [jnp.sum(picked[1 + r], axis=0, keepdims=True) for r in range(1, 8)]
    cnt_r += [picked[9][j:j + 1] for j in range(SUBLANES)]

    cnt_r = [jnp.broadcast_to(c, rows) for c in cnt_r]
    rank2, cnt_k, coef, e2 = [], [], [], []
    inv_z = 1.0 / z
    for blk1, blk2 in zip(b1, b2):
        r2 = jnp.full(rows, float(PEER_TOPK), F32)
        ck = jnp.zeros(rows, F32)
        for r in range(PEER_TOPK - 1, -1, -1):
            r2 = jnp.where(blk2 >= v2[r], float(r), r2)
            ck = jnp.where(blk1 == v1[r], cnt_r[r], ck)
        rank2.append(r2)
        cnt_k.append(ck)
        coef.append(jnp.exp(blk1 - v1[0]) * inv_z)
        e2.append(jnp.exp(blk2 - v2[0]))

    strict = [v[r] > v[r + 1] for v in (v1, v2) for r in range(PEER_TOPK - 1)]
    ok = _count(strict) == float(SUBLANES * len(strict))
    for blocks, vals in ((b1, v1), (b2, v2), (cand, best)):
        ok = ok & (_count([b >= vals[PEER_TOPK - 1] for b in blocks]) == float(PEER_TOPK))
    redo = jnp.max(jnp.where(ok, 0.0, 1.0))
    cat = lambda parts: jnp.concatenate(parts, axis=0)
    return cat(rank2), cat(e2), cat(coef), cat(cnt_k), redo


def _top16_rows_any(s):
    idx = lax.broadcasted_iota(jnp.int32, s.shape, 0)
    rank = jnp.full(s.shape, float(PEER_TOPK), F32)
    vals = []
    for r in range(PEER_TOPK):
        m = jnp.max(s, axis=0, keepdims=True)
        sel = idx == jnp.min(jnp.where(s == m, idx, s.shape[0]), axis=0, keepdims=True)
        rank = jnp.where(sel, float(r), rank)
        s = jnp.where(sel, -jnp.inf, s)
        vals.append(m)
    return rank, jnp.concatenate(vals, axis=0)


def _route_head_any(s1, s2):
    rank1, v1 = _top16_rows_any(s1)
    rank2, v2 = _top16_rows_any(s2)
    blocks = [v1[0:1] + v2]
    for r1 in range(1, 8):
        blocks.append(v1[r1:r1 + 1] + v2[0:8])
    blocks.append(v1[8:16] + v2[0:1])
    cand = jnp.concatenate(blocks, axis=0)
    rank3, _ = _top16_rows_any(cand)
    chosen = jnp.where(rank3 < PEER_TOPK, 1.0, 0.0)
    z = jnp.sum(chosen * jnp.exp(cand - cand[0:1]), axis=0, keepdims=True)
    counts = [jnp.sum(chosen[0:16], axis=0, keepdims=True)]
    for r1 in range(1, 8):
        counts.append(jnp.sum(chosen[8 + 8 * r1: 16 + 8 * r1], axis=0, keepdims=True))
    counts.append(chosen[N_CAND_ROWS - 8:])
    cnt_r = jnp.concatenate(counts, axis=0)
    cnt_k = jnp.zeros(s1.shape, F32)
    for r in range(PEER_TOPK):
        cnt_k = jnp.where(rank1 == r, cnt_r[r:r + 1], cnt_k)
    return rank2, jnp.exp(s2 - v2[0:1]), jnp.exp(s1 - v1[0:1]) / z, cnt_k


def _router_kernel(h_ref, wq_ref, k1_ref, k2_ref, rank2_ref, e2_ref, coef_ref, cnt_ref):
    hb = h_ref[...]

    def emit(h, rank2, e2, coef, cnt_k):
        rank2_ref[h] = rank2.astype(BF16)
        e2_ref[h] = e2.astype(BF16)
        coef_ref[h] = coef
        cnt_ref[h] = cnt_k

    def heads(g, carry):
        todo = []
        for i in range(ROUTER_HEADS_PER_TRIP):
            h = g * ROUTER_HEADS_PER_TRIP + i
            q_t = lax.dot_general(wq_ref[h], hb, (((1,), (1,)), ((), ())), preferred_element_type=F32)
            s1 = jnp.dot(k1_ref[...], q_t[:PEER_HALF].astype(BF16), preferred_element_type=F32)
            s2 = jnp.dot(k2_ref[...], q_t[PEER_HALF:].astype(BF16), preferred_element_type=F32)
            *stats, redo = _route_head(s1, s2)
            emit(h, *stats)
            todo.append((h, s1, s2, redo))
        for h, s1, s2, redo in todo:
            pl.when(redo > 0.0)(functools.partial(lambda h, s1, s2: emit(h, *_route_head_any(s1, s2)), h, s1, s2))
        return carry

    lax.fori_loop(0, PEER_HEADS // ROUTER_HEADS_PER_TRIP, heads, 0)


def _router_call(hb, wq_t, k1, k2):
    T, D = hb.shape
    tm = ROUTER_TILE
    o_spec = pl.BlockSpec((PEER_HEADS, PEER_NKEYS, tm), lambda t: (0, 0, t))
    shape = lambda dt: jax.ShapeDtypeStruct((PEER_HEADS, PEER_NKEYS, T), dt)
    return pl.pallas_call(
        _router_kernel,
        grid=(T // tm,),
        in_specs=[pl.BlockSpec((tm, D), lambda t: (t, 0)),
                  _resident(wq_t.shape), _resident(k1.shape), _resident(k2.shape)],
        out_specs=[o_spec] * 4,
        out_shape=[shape(BF16), shape(BF16), shape(F32), shape(F32)],
        compiler_params=_params(("parallel",)),
        name="peer_router",
    )(hb, wq_t, k1, k2)


PREP_TILE = 512


def _uvprep_kernel(u_ref, v_ref, ub_ref, vt_ref):
    ub_ref[...] = u_ref[...].astype(BF16)
    vt_ref[...] = v_ref[...].T.astype(BF16)


def _uvprep_call(u, v):
    E, D = u.shape
    te = PREP_TILE
    return pl.pallas_call(
        _uvprep_kernel,
        grid=(E // te,),
        in_specs=[pl.BlockSpec((te, D), lambda e: (e, 0)), pl.BlockSpec((te, D), lambda e: (e, 0))],
        out_specs=[pl.BlockSpec((te, D), lambda e: (e, 0)), pl.BlockSpec((D, te), lambda e: (0, e))],
        out_shape=[jax.ShapeDtypeStruct((E, D), BF16), jax.ShapeDtypeStruct((D, E), BF16)],
        compiler_params=_params(("parallel",)),
        name="peer_uvprep",
    )(u, v)


PEER_TOK_TILE = 512
PEER_EXP_TILE = 1024
KEYS_PER_STEP = PEER_EXP_TILE // PEER_NKEYS
PEER_STAGES = 3


def _peer_kernel(h_ref, u_ref, vt_ref, rank2_ref, e2_ref, coef_ref, cnt_ref, o_ref, act_ref, p_ref, *, n_exp_tiles):
    s = pl.program_id(0)
    cur = s % 2
    prev = (s + 1) % 2

    @pl.when(s == 0)
    def _():
        act_ref[1] = jnp.zeros(act_ref.shape[1:], BF16)
        p_ref[0] = jnp.zeros(p_ref.shape[1:], BF16)

    @pl.when(jnp.maximum(s - (PEER_STAGES - 1), 0) % n_exp_tiles == 0)
    def _():
        o_ref[...] = jnp.zeros(o_ref.shape, F32)

    act_ref[cur] = lax.dot_general(u_ref[...], h_ref[...], (((1,), (1,)), ((), ())),
                                   preferred_element_type=F32).astype(BF16)

    for key in range(KEYS_PER_STEP):
        gate = None
        for h in range(PEER_HEADS):
            cnt = cnt_ref[h, key:key + 1, :].astype(BF16)
            cf = coef_ref[h, key:key + 1, :].astype(BF16)
            term = jnp.where(rank2_ref[h] < cnt, e2_ref[h], jnp.zeros((), BF16)) * cf
            gate = term if gate is None else gate + term
        rows = slice(key * PEER_NKEYS, (key + 1) * PEER_NKEYS)
        p_ref[prev, rows, :] = gate * _gelu_tanh(act_ref[prev, rows, :])

    o_ref[...] += jnp.dot(vt_ref[...], p_ref[cur], preferred_element_type=F32)


def _peer_call(hb, ub, vt, rank2, e2, coef, cnt):
    T, D = hb.shape
    tm, te = PEER_TOK_TILE, PEER_EXP_TILE
    n_e = PEER_EXPERTS // te
    n_pairs = (T // tm) * n_e
    by_tile = lambda a: a.reshape(PEER_HEADS, n_e, KEYS_PER_STEP, T)

    def pair(s, lag):
        i = jnp.clip(s - lag, 0, n_pairs - 1)
        return i // n_e, i % n_e

    head_spec = pl.BlockSpec((PEER_HEADS, PEER_NKEYS, tm), lambda s: (0, 0, pair(s, 1)[0]))
    key_spec = pl.BlockSpec((PEER_HEADS, None, KEYS_PER_STEP, tm),
                            lambda s: (0, pair(s, 1)[1], 0, pair(s, 1)[0]))
    return pl.pallas_call(
        functools.partial(_peer_kernel, n_exp_tiles=n_e),
        grid=(n_pairs + PEER_STAGES - 1,),
        in_specs=[pl.BlockSpec((tm, D), lambda s: (pair(s, 0)[0], 0)),
                  pl.BlockSpec((te, D), lambda s: (pair(s, 0)[1], 0)),
                  pl.BlockSpec((D, te), lambda s: (0, pair(s, 2)[1])),
                  head_spec, head_spec, key_spec, key_spec],
        out_specs=pl.BlockSpec((D, tm), lambda s: (0, pair(s, 2)[0])),
        out_shape=jax.ShapeDtypeStruct((D, T), F32),
        scratch_shapes=[pltpu.VMEM((2, te, tm), BF16), pltpu.VMEM((2, te, tm), BF16)],
        compiler_params=_params(("arbitrary",)),
        name="peer_dense",
    )(hb, ub, vt, rank2, e2, by_tile(coef), by_tile(cnt))


FINAL_TILE = 512


def _final_kernel(h_ref, yt_ref, g_ref, b_ref, o_ref, *, depth):
    o_ref[...] = _layer_norm(_deepnorm_alpha(depth) * h_ref[...] + yt_ref[...].T, g_ref[...], b_ref[...])


def _final_call(h1, y_t, ln_g, ln_b, depth):
    T, D = h1.shape
    tm = FINAL_TILE
    return pl.pallas_call(
        functools.partial(_final_kernel, depth=depth),
        grid=(T // tm,),
        in_specs=[pl.BlockSpec((tm, D), lambda t: (t, 0)),
                  pl.BlockSpec((D, tm), lambda t: (0, t)),
                  _resident((1, D)), _resident((1, D))],
        out_specs=pl.BlockSpec((tm, D), lambda t: (t, 0)),
        out_shape=jax.ShapeDtypeStruct((T, D), F32),
        compiler_params=_params(("parallel",)),
        name="final_ln",
    )(h1, y_t, ln_g.reshape(1, D), ln_b.reshape(1, D))


def _mixer(h, positions, w_in, conv_w, conv_b, rg_w_a, rg_b_a, rg_w_x, rg_b_x, lru_lambda,
           gn_attn, gn_rec, w_out, ln1_g, ln1_b, peer_wq, peer_u, peer_v, depth):
    B, S, D = h.shape
    half = HEAD_DIM // 2
    inv = ROPE_THETA ** (-jnp.arange(half, dtype=F32) / half)
    inv_full = jnp.concatenate([inv, inv]).reshape(1, HEAD_DIM)
    w_qkv = w_in[:, :3 * ATTN_WIDTH].astype(BF16)
    wax = jnp.concatenate([rg_w_a, rg_w_x], axis=-1).astype(BF16)
    qkv_steps = B * (S // QKV_TILE)
    halves = []
    if (3 * ATTN_WIDTH) % REC_WIDTH == 0 and D % qkv_steps == 0:
        step = lambda b, s: b * (S // QKV_TILE) + s
        for i in range(2):
            col = 3 * ATTN_WIDTH // REC_WIDTH + i
            halves.append(_Rider(w_in, (D // qkv_steps, REC_WIDTH), lambda b, s, col=col: (step(b, s), col),
                                 (D, REC_WIDTH), (D // qkv_steps, REC_WIDTH), lambda b, s: (step(b, s), 0), False))
    if halves and all(r.fits for r in halves):
        q, k, v, w_x, w_g = _qkv_call(h, positions.reshape(B, S, 1), inv_full, w_qkv, halves)
    else:
        q, k, v = _qkv_call(h, positions.reshape(B, S, 1), inv_full, w_qkv)
        w_x = w_in[:, 3 * ATTN_WIDTH:3 * ATTN_WIDTH + REC_WIDTH].astype(BF16)
        w_g = w_in[:, 3 * ATTN_WIDTH + REC_WIDTH:].astype(BF16)
    rec_n = _rec_call(h, w_x, w_g, conv_w, conv_b, wax, rg_b_a, rg_b_x, lru_lambda, gn_rec)
    n_steps = B * ATTN_HEADS
    wq_rider = None
    if PEER_HEADS == ATTN_HEADS and D % B == 0:
        wq_rider = _Rider(peer_wq, (D // B, 2 * PEER_HALF), lambda b, h: (b, h),
                          (PEER_HEADS, 2 * PEER_HALF, D), (None, 2 * PEER_HALF, D // B), lambda b, h: (h, 0, b), True)
    wanted = {"w_out": _row_slices(w_out, n_steps, ATTN_HEADS, False), "wq_t": wq_rider,
              "ub": _row_slices(peer_u, n_steps, ATTN_HEADS, False),
              "vt": _row_slices(peer_v, n_steps, ATTN_HEADS, True)}
    names = [n for n, r in wanted.items() if r is not None and r.fits]
    attn, *converted = _attn_call(q, k, v, [wanted[n] for n in names])
    done = dict(zip(names, converted))
    w_out_b = done["w_out"] if "w_out" in done else w_out.astype(BF16)
    h1, h1b = _mix_call(h, attn, rec_n, w_out_b, gn_attn, ln1_g, ln1_b, depth)
    return h1, h1b, done


def _peer(h, hb, wq, keys1, keys2, u, v, done, ln2_g, ln2_b, depth):
    B, S, D = h.shape
    h2 = h.reshape(B * S, D)
    hb2 = hb.reshape(B * S, D)
    if "wq_t" in done:
        wq_t = done["wq_t"]
    else:
        wq_t = wq.reshape(D, PEER_HEADS, 2 * PEER_HALF).transpose(1, 2, 0).astype(BF16)
    rank2, e2, coef, cnt = _router_call(hb2, wq_t, keys1.astype(BF16), keys2.astype(BF16))
    if "ub" in done and "vt" in done:
        ub, vt = done["ub"], done["vt"]
    else:
        ub, vt = _uvprep_call(u, v)
    y_t = _peer_call(hb2, ub, vt, rank2, e2, coef, cnt)
    return _final_call(h2, y_t, ln2_g, ln2_b, depth).reshape(B, S, D)


def kernel(x, positions, w_in, conv_w, conv_b, rg_w_a, rg_b_a, rg_w_x, rg_b_x, lru_lambda, gn_attn, gn_rec,
           w_out, ln1_g, ln1_b, peer_wq, peer_keys1, peer_keys2, peer_u, peer_v, ln2_g, ln2_b):
    depth = w_in.shape[0]
    h = x
    for l in range(depth):
        h, hb, done = _mixer(h, positions, w_in[l], conv_w[l], conv_b[l], rg_w_a[l], rg_b_a[l], rg_w_x[l],
                             rg_b_x[l], lru_lambda[l], gn_attn[l], gn_rec[l], w_out[l], ln1_g[l], ln1_b[l],
                             peer_wq[l], peer_u[l], peer_v[l], depth)
        h = _peer(h, hb, peer_wq[l], peer_keys1[l], peer_keys2[l], peer_u[l], peer_v[l], done, ln2_g[l],
                  ln2_b[l], depth)
    return h
```

```python
import functools
import math

import jax
import jax.numpy as jnp
import numpy as np
from jax import lax
from jax.experimental import pallas as pl
from jax.experimental.pallas import tpu as pltpu

F32 = jnp.float32
BF16 = jnp.bfloat16

ATTN_HEADS = 8
HEAD_DIM = 128
ATTN_WIDTH = ATTN_HEADS * HEAD_DIM
REC_BLOCKS = 8
REC_BLOCK = 128
REC_WIDTH = REC_BLOCKS * REC_BLOCK
CONV_WIDTH = 4
LRU_C = 8.0
BAND = 128
DILATIONS = (1, 4, 16)
ROPE_THETA = 10000.0
NEG_INF = -1e30
PEER_HEADS = 8
PEER_NKEYS = 128
PEER_EXPERTS = PEER_NKEYS * PEER_NKEYS
PEER_HALF = 128
PEER_TOPK = 16
SUBLANES = 8

V7X_VMEM_LIMIT = 56 * 1024 * 1024


def _params(sem, vmem=V7X_VMEM_LIMIT):
    return pltpu.CompilerParams(dimension_semantics=sem, vmem_limit_bytes=vmem)


def _resident(shape):
    zeros = (0,) * len(shape)
    return pl.BlockSpec(shape, lambda *_: zeros, pipeline_mode=pl.Buffered(1))


def _exact(v):
    assert float(np.float32(v)) == float(v)
    return jnp.full((1, 1), v, F32)


_PI_HI = float(np.float32(math.pi))
_PI_MID = float(np.float32(math.pi - _PI_HI))
_PI_LO = float(np.float32(math.pi - _PI_HI - _PI_MID))


def _ln_eps():
    return _exact(1.0) / _exact(1e5)


def _gelu_tanh(x):
    pi = _exact(_PI_HI) + _exact(_PI_MID) + _exact(_PI_LO)
    c = jnp.sqrt(_exact(2.0) / pi).astype(x.dtype)
    k = (_exact(44715.0) / _exact(1e6)).astype(x.dtype)
    return 0.5 * x * (1.0 + jnp.tanh(c * (x + k * (x * x * x))))


def _sigmoid(x):
    return 1.0 / (1.0 + jnp.exp(-x))


def _rms_scale(x):
    return lax.rsqrt(jnp.mean(x * x, axis=-1, keepdims=True) + _ln_eps())


def _layer_norm(z, g, b):
    mu = jnp.mean(z, axis=-1, keepdims=True)
    zc = z - mu
    var = jnp.mean(zc * zc, axis=-1, keepdims=True)
    return zc * lax.rsqrt(var + _ln_eps()) * g + b


def _deepnorm_alpha(depth):
    return jnp.sqrt(jnp.sqrt(_exact(2.0 * depth)))


QKV_TILE = 512


def _qkv_kernel(x_ref, pos_ref, inv_ref, w_ref, *rest, n_riders):
    srcs, (q_ref, k_ref, v_ref), dsts = rest[:n_riders], rest[n_riders:n_riders + 3], rest[n_riders + 3:]
    for src, dst in zip(srcs, dsts):
        dst[...] = src[...].astype(BF16)
    proj = jnp.dot(x_ref[0].astype(BF16), w_ref[...], preferred_element_type=F32)
    ang = pos_ref[0].astype(F32) * inv_ref[...]
    first = lax.broadcasted_iota(jnp.int32, ang.shape, 1) < HEAD_DIM // 2
    half_pi = 0.5 * (_exact(_PI_HI) + _exact(_PI_MID) + _exact(_PI_LO))
    both = jnp.sin(ang + jnp.where(first, half_pi, 0.0))
    swapped = pltpu.roll(both, HEAD_DIM // 2, 1)
    cos = jnp.where(first, both, swapped)
    sin = jnp.where(first, -swapped, both)
    scale = lax.rsqrt(_exact(float(HEAD_DIM)))
    for h in range(ATTN_HEADS):
        cols = slice(h * HEAD_DIM, (h + 1) * HEAD_DIM)
        qh = proj[:, cols]
        kh = proj[:, ATTN_WIDTH + h * HEAD_DIM: ATTN_WIDTH + (h + 1) * HEAD_DIM]
        q_ref[0, :, cols] = (qh * cos + pltpu.roll(qh, HEAD_DIM // 2, 1) * sin) * scale
        k_ref[0, :, cols] = kh * cos + pltpu.roll(kh, HEAD_DIM // 2, 1) * sin
    v_ref[0] = proj[:, 2 * ATTN_WIDTH:]


def _qkv_call(x, pos, inv_full, w_qkv, riders=()):
    B, S, D = x.shape
    tm = QKV_TILE
    spec = pl.BlockSpec((1, tm, ATTN_WIDTH), lambda b, s: (b, s, 0))
    shape = jax.ShapeDtypeStruct((B, S, ATTN_WIDTH), F32)
    assert not any(r.transposed for r in riders)
    return pl.pallas_call(
        functools.partial(_qkv_kernel, n_riders=len(riders)),
        grid=(B, S // tm),
        in_specs=[pl.BlockSpec((1, tm, D), lambda b, s: (b, s, 0)),
                  pl.BlockSpec((1, tm, 1), lambda b, s: (b, s, 0)),
                  _resident((1, HEAD_DIM)),
                  _resident(w_qkv.shape)] + [r.in_spec for r in riders],
        out_specs=[spec, spec, spec] + [r.out_spec for r in riders],
        out_shape=[shape, shape, shape] + [r.out_shape for r in riders],
        compiler_params=_params(("parallel", "parallel")),
        name="qkv_rope",
    )(x, pos, inv_full, w_qkv, *[r.src for r in riders])


REC_TILE = 512


def _rec_kernel(x_ref, wx_ref, wg_ref, cw_ref, cb_ref, wax_ref, ba_ref, bx_ref, lam_ref, g_ref, o_ref,
                ext_ref, a_ref, b_ref, h_ref, hc_ref):
    tm = REC_TILE

    @pl.when(pl.program_id(1) == 0)
    def _():
        ext_ref[0:SUBLANES, :] = jnp.zeros((SUBLANES, REC_WIDTH), F32)
        hc_ref[...] = jnp.zeros((SUBLANES, REC_WIDTH), F32)

    xb = x_ref[0].astype(BF16)
    xr = jnp.dot(xb, wx_ref[...], preferred_element_type=F32)
    gr = jnp.dot(xb, wg_ref[...], preferred_element_type=F32)
    ext_ref[SUBLANES:2 * SUBLANES, :] = xr[:SUBLANES, :]
    xc = cb_ref[...] + cw_ref[CONV_WIDTH - 1:CONV_WIDTH, :] * xr
    xc_head = xc[:SUBLANES, :]
    for j in range(CONV_WIDTH - 1):
        back = CONV_WIDTH - 1 - j
        xc = xc + cw_ref[j:j + 1, :] * pltpu.roll(xr, back, 0)
        xc_head = xc_head + cw_ref[j:j + 1, :] * ext_ref[pl.ds(SUBLANES - back, SUBLANES), :]
    xc = jnp.concatenate([xc_head, xc[SUBLANES:, :]], axis=0)
    ext_ref[0:SUBLANES, :] = xr[tm - SUBLANES:, :]

    rs, is_ = [], []
    for n in range(REC_BLOCKS):
        xb = xc[:, n * REC_BLOCK:(n + 1) * REC_BLOCK].astype(BF16)
        gts = jnp.dot(xb, wax_ref[n], preferred_element_type=F32)
        rs.append(gts[:, :REC_BLOCK])
        is_.append(gts[:, REC_BLOCK:])
    r = _sigmoid(jnp.concatenate(rs, axis=1) + ba_ref[...])
    ig = _sigmoid(jnp.concatenate(is_, axis=1) + bx_ref[...])
    z = -lam_ref[...]
    softplus = jnp.maximum(z, 0.0) + jnp.log(1.0 + jnp.exp(-jnp.abs(z)))
    log_a = (-LRU_C) * r * softplus
    a = jnp.exp(log_a)
    gap = 1.0 - jnp.exp(2.0 * log_a)
    bt = jnp.where(gap > 0.0, gap * lax.rsqrt(gap), 0.0) * (ig * xc)

    r8 = lax.broadcasted_iota(jnp.int32, a.shape, 0) & (SUBLANES - 1)
    for s in (1, 2, 4):
        a_sh = pltpu.roll(a, s, 0)
        b_sh = pltpu.roll(bt, s, 0)
        m = r8 >= s
        bt = jnp.where(m, a * b_sh + bt, bt)
        a = jnp.where(m, a * a_sh, a)
    a_ref[...] = a
    b_ref[...] = bt

    def group(g, hc):
        rows = pl.ds(pl.multiple_of(g * SUBLANES, SUBLANES), SUBLANES)
        h = a_ref[rows, :] * hc + b_ref[rows, :]
        h_ref[rows, :] = h
        return jnp.broadcast_to(h[SUBLANES - 1:SUBLANES, :], (SUBLANES, REC_WIDTH))

    hc_ref[...] = lax.fori_loop(0, tm // SUBLANES, group, hc_ref[...], unroll=8)

    rec = h_ref[...] * _gelu_tanh(gr)
    o_ref[0] = (rec * _rms_scale(rec) * g_ref[...]).astype(BF16)


def _rec_call(x, w_x, w_g, conv_w, conv_b, wax, b_a, b_x, lam, gn_rec):
    B, S, D = x.shape
    tm = REC_TILE
    row = lambda a: a.reshape(1, REC_WIDTH)
    return pl.pallas_call(
        _rec_kernel,
        grid=(B, S // tm),
        in_specs=[pl.BlockSpec((1, tm, D), lambda b, s: (b, s, 0)),
                  _resident(w_x.shape), _resident(w_g.shape),
                  _resident((CONV_WIDTH, REC_WIDTH)),
                  _resident((1, REC_WIDTH)),
                  _resident(wax.shape),
                  _resident((1, REC_WIDTH)), _resident((1, REC_WIDTH)),
                  _resident((1, REC_WIDTH)), _resident((1, REC_WIDTH))],
        out_specs=pl.BlockSpec((1, tm, REC_WIDTH), lambda b, s: (b, s, 0)),
        out_shape=jax.ShapeDtypeStruct((B, S, REC_WIDTH), BF16),
        scratch_shapes=[pltpu.VMEM((2 * SUBLANES, REC_WIDTH), F32),
                        pltpu.VMEM((tm, REC_WIDTH), F32),
                        pltpu.VMEM((tm, REC_WIDTH), F32),
                        pltpu.VMEM((tm, REC_WIDTH), F32),
                        pltpu.VMEM((SUBLANES, REC_WIDTH), F32)],
        compiler_params=_params(("parallel", "arbitrary")),
        name="rec_branch",
    )(x, w_x, w_g, conv_w, row(conv_b), wax, row(b_a), row(b_x), row(lam), row(gn_rec))


ATTN_UNROLL = 16


def _band_block(q, k, v, valid):
    s = lax.dot_general(q.astype(BF16), k.astype(BF16), (((1,), (1,)), ((), ())), preferred_element_type=F32)
    s = jnp.where(valid, s, NEG_INF)
    m = jnp.max(s, axis=-1, keepdims=True)
    p = jnp.exp(s - m)
    l = jnp.sum(p, axis=-1, keepdims=True)
    acc = jnp.dot(p.astype(BF16), v.astype(BF16), preferred_element_type=F32)
    return acc, m, l


def _attn_kernel(q_ref, k_ref, v_ref, *rest, seq, transposed):
    n = len(transposed)
    srcs, o_ref, dsts, stats = rest[:n], rest[n], rest[n + 1:2 * n + 1], rest[2 * n + 1:]
    for src, dst, flip in zip(srcs, dsts, transposed):
        dst[...] = (src[...].T if flip else src[...]).astype(BF16)
    wide = (BAND, HEAD_DIM)

    for p, dil in enumerate(DILATIONS):
        acc_ref, m_ref, l_ref = stats[3 * p: 3 * p + 3]
        n_blk = seq // (dil * BAND)

        def block(idx, carry, dil=dil, n_blk=n_blk, acc_ref=acc_ref, m_ref=m_ref, l_ref=l_ref):
            c = idx // n_blk
            j = idx - c * n_blk
            k0 = jnp.maximum(j - 1, 0) * BAND
            n_keys = 2 * BAND if n_blk > 1 else BAND
            if dil == 1:
                rows = pl.ds(pl.multiple_of(j * BAND, BAND), BAND)
                krows = pl.ds(pl.multiple_of(k0, BAND), n_keys)
            else:
                rows = pl.ds(dil * j * BAND + c, BAND, stride=dil)
                krows = pl.ds(dil * k0 + c, n_keys, stride=dil)
            qi = lax.broadcasted_iota(jnp.int32, (BAND, n_keys), 0)
            kj = lax.broadcasted_iota(jnp.int32, (BAND, n_keys), 1)
            dist = (j * BAND + qi) - (k0 + kj)
            acc, m, l = _band_block(q_ref[0, rows, :], k_ref[0, krows, :], v_ref[0, krows, :],
                                    (dist >= 0) & (dist <= BAND))
            acc_ref[rows, :] = acc
            m_ref[rows, :] = jnp.broadcast_to(m, wide)
            l_ref[rows, :] = jnp.broadcast_to(l, wide)
            return carry

        lax.fori_loop(0, seq // BAND, block, 0, unroll=ATTN_UNROLL)

    def merge(j, carry):
        rows = pl.ds(pl.multiple_of(j * BAND, BAND), BAND)
        ms = [stats[3 * p + 1][rows, :] for p in range(3)]
        mx = jnp.maximum(jnp.maximum(ms[0], ms[1]), ms[2])
        ws = [jnp.exp(mm - mx) for mm in ms]
        num = sum(ws[p] * stats[3 * p][rows, :] for p in range(3))
        den = sum(ws[p] * stats[3 * p + 2][rows, :] for p in range(3))
        o_ref[0, rows, :] = num / den
        return carry

    lax.fori_loop(0, seq // BAND, merge, 0, unroll=ATTN_UNROLL)


ATTN_RIDER_BYTES_MAX = 4 * 1024 * 1024


class _Rider:
    def __init__(self, src, in_block, in_index, out_shape, out_block, out_index, transposed):
        self.src, self.transposed = src, transposed
        self.in_spec = pl.BlockSpec(in_block, in_index)
        self.out_spec = pl.BlockSpec(out_block, out_index)
        self.out_shape = jax.ShapeDtypeStruct(out_shape, BF16)
        rows, cols = [d for d in in_block if d is not None]
        self.fits = rows % 16 == 0 and cols % HEAD_DIM == 0 and rows * cols * 4 <= ATTN_RIDER_BYTES_MAX
        if transposed:
            self.fits = self.fits and rows % HEAD_DIM == 0


def _row_slices(w, n_steps, n_heads, transposed):
    rows, cols = w.shape
    r = rows // n_steps
    step = lambda b, h: b * n_heads + h
    if r * n_steps != rows:
        return None
    if transposed:
        return _Rider(w, (r, cols), lambda b, h: (step(b, h), 0), (cols, rows), (cols, r),
                      lambda b, h: (0, step(b, h)), True)
    return _Rider(w, (r, cols), lambda b, h: (step(b, h), 0), (rows, cols), (r, cols),
                  lambda b, h: (step(b, h), 0), False)


def _attn_call(q, k, v, riders=()):
    B, S, _ = q.shape
    assert S % (max(DILATIONS) * BAND) == 0
    spec = pl.BlockSpec((1, S, HEAD_DIM), lambda b, h: (b, 0, h))
    stat = pltpu.VMEM((S, HEAD_DIM), F32)
    outs = pl.pallas_call(
        functools.partial(_attn_kernel, seq=S, transposed=tuple(r.transposed for r in riders)),
        grid=(B, ATTN_HEADS),
        in_specs=[spec] * 3 + [r.in_spec for r in riders],
        out_specs=[spec] + [r.out_spec for r in riders],
        out_shape=[jax.ShapeDtypeStruct((B, S, ATTN_WIDTH), F32)] + [r.out_shape for r in riders],
        scratch_shapes=[stat] * 9,
        compiler_params=_params(("parallel", "parallel")),
        name="dilated_attn",
    )(q, k, v, *[r.src for r in riders])
    return tuple(outs)


MIX_TILE = 512


def _mix_kernel(x_ref, at_ref, rc_ref, wo_ref, ga_ref, g_ref, b_ref, o_ref, ob_ref, *, depth):
    at = at_ref[0]
    at_n = (at * _rms_scale(at) * ga_ref[...]).astype(BF16)
    y = jnp.dot(at_n, wo_ref[:ATTN_WIDTH, :], preferred_element_type=F32)
    y = y + jnp.dot(rc_ref[0], wo_ref[ATTN_WIDTH:, :], preferred_element_type=F32)
    h = _layer_norm(_deepnorm_alpha(depth) * x_ref[0] + y, g_ref[...], b_ref[...])
    o_ref[0] = h
    ob_ref[0] = h.astype(BF16)


def _mix_call(x, attn, rec_n, w_out, gn_attn, ln_g, ln_b, depth):
    B, S, D = x.shape
    tm = MIX_TILE
    row = lambda a, n: a.reshape(1, n)
    tile = lambda n: pl.BlockSpec((1, tm, n), lambda b, s: (b, s, 0))
    return pl.pallas_call(
        functools.partial(_mix_kernel, depth=depth),
        grid=(B, S // tm),
        in_specs=[tile(D), tile(ATTN_WIDTH), tile(REC_WIDTH),
                  _resident(w_out.shape),
                  _resident((1, ATTN_WIDTH)), _resident((1, D)), _resident((1, D))],
        out_specs=[tile(D), tile(D)],
        out_shape=[jax.ShapeDtypeStruct((B, S, D), F32), jax.ShapeDtypeStruct((B, S, D), BF16)],
        compiler_params=_params(("parallel", "parallel")),
        name="mix_out_ln",
    )(x, attn, rec_n, w_out, row(gn_attn, ATTN_WIDTH), row(ln_g, D), row(ln_b, D))


ROUTER_TILE = 1024
N_CAND_ROWS = 16 + 7 * 8 + 8
ROUTER_HEADS_PER_TRIP = 1


def _sort_network(n):
    def merge(lo, hi, r):
        step = r * 2
        if step < hi - lo:
            yield from merge(lo, hi, step)
            yield from merge(lo + r, hi, step)
            yield from ((i, i + r) for i in range(lo + r, hi - r, step))
        else:
            yield (lo, lo + r)

    def sort(lo, hi):
        if hi - lo >= 1:
            mid = lo + (hi - lo) // 2
            yield from sort(lo, mid)
            yield from sort(mid + 1, hi)
            yield from merge(lo, hi, 1)

    return tuple(sort(0, n - 1))


_SORT16 = _sort_network(PEER_TOPK)
_BITONIC16 = tuple((i, i + d) for d in (8, 4, 2, 1) for i in range(PEER_TOPK) if i & d == 0)


def _exchange(x, pairs):
    for i, j in pairs:
        a, b = x[i], x[j]
        if a is None:
            x[i], x[j] = b, None
        elif b is not None:
            x[i], x[j] = jnp.maximum(a, b), jnp.minimum(a, b)
    return x


def _top16_sorted(blocks):
    x = _exchange(list(blocks) + [None] * (PEER_TOPK - len(blocks)), _SORT16)
    for shift in (4, 2, 1):
        partner = [None if v is None else pltpu.roll(v, shift, 0) for v in x]
        merged = []
        for i in range(PEER_TOPK):
            a, b = x[i], partner[PEER_TOPK - 1 - i]
            merged.append(b if a is None else a if b is None else jnp.maximum(a, b))
        x = _exchange(merged, _BITONIC16)
    return x


def _sublane_blocks(a):
    return [a[i * SUBLANES:(i + 1) * SUBLANES] for i in range(a.shape[0] // SUBLANES)]


def _count(flags):
    total = sum(jnp.where(f, 1.0, 0.0) for f in flags)
    return jnp.sum(total, axis=0, keepdims=True)


def _route_head(s1, s2):
    b1, b2 = _sublane_blocks(s1), _sublane_blocks(s2)
    v1, v2 = _top16_sorted(b1), _top16_sorted(b2)
    rows = (v1[0].shape[0], v1[0].shape[1])
    sub = lax.broadcasted_iota(jnp.int32, rows, 0)

    def on_sublanes(vals):
        out = vals[0]
        for j in range(1, SUBLANES):
            out = jnp.where(sub == j, vals[j], out)
        return out

    v2_lo, v2_hi, v1_hi = on_sublanes(v2[:8]), on_sublanes(v2[8:]), on_sublanes(v1[8:])
    cand = [v1[0] + v2_lo, v1[0] + v2_hi] + [v1[r] + v2_lo for r in range(1, 8)] + [v1_hi + v2[0]]
    best = _top16_sorted(cand)
    chosen = [c >= best[PEER_TOPK - 1] for c in cand]
    picked = [jnp.where(f, 1.0, 0.0) for f in chosen]
    z = jnp.sum(sum(p * jnp.exp(c - best[0]) for p, c in zip(picked, cand)), axis=0, keepdims=True)
    cnt_r = [jnp.sum(picked[0] + picked[1], axis=0, keepdims=True)]
    cnt_r += [jnp.sum(picked[1 + r], axis=0, keepdims=True) for r in range(1, 8)]
    cnt_r += [picked[9][j:j + 1] for j in range(SUBLANES)]

    cnt_r = [jnp.broadcast_to(c, rows) for c in cnt_r]
    rank2, cnt_k, coef, e2 = [], [], [], []
    inv_z = 1.0 / z
    for blk1, blk2 in zip(b1, b2):
        r2 = jnp.full(rows, float(PEER_TOPK), F32)
        ck = jnp.zeros(rows, F32)
        for r in range(PEER_TOPK - 1, -1, -1):
            r2 = jnp.where(blk2 >= v2[r], float(r), r2)
            ck = jnp.where(blk1 == v1[r], cnt_r[r], ck)
        rank2.append(r2)
        cnt_k.append(ck)
        coef.append(jnp.exp(blk1 - v1[0]) * inv_z)
        e2.append(jnp.exp(blk2 - v2[0]))

    strict = [v[r] > v[r + 1] for v in (v1, v2) for r in range(PEER_TOPK - 1)]
    ok = _count(strict) == float(SUBLANES * len(strict))
    for blocks, vals in ((b1, v1), (b2, v2), (cand, best)):
        ok = ok & (_count([b >= vals[PEER_TOPK - 1] for b in blocks]) == float(PEER_TOPK))
    redo = jnp.max(jnp.where(ok, 0.0, 1.0))
    cat = lambda parts: jnp.concatenate(parts, axis=0)
    return cat(rank2), cat(e2), cat(coef), cat(cnt_k), redo


def _top16_rows_any(s):
    idx = lax.broadcasted_iota(jnp.int32, s.shape, 0)
    rank = jnp.full(s.shape, float(PEER_TOPK), F32)
    vals = []
    for r in range(PEER_TOPK):
        m = jnp.max(s, axis=0, keepdims=True)
        sel = idx == jnp.min(jnp.where(s == m, idx, s.shape[0]), axis=0, keepdims=True)
        rank = jnp.where(sel, float(r), rank)
        s = jnp.where(sel, -jnp.inf, s)
        vals.append(m)
    return rank, jnp.concatenate(vals, axis=0)


def _route_head_any(s1, s2):
    rank1, v1 = _top16_rows_any(s1)
    rank2, v2 = _top16_rows_any(s2)
    blocks = [v1[0:1] + v2]
    for r1 in range(1, 8):
        blocks.append(v1[r1:r1 + 1] + v2[0:8])
    blocks.append(v1[8:16] + v2[0:1])
    cand = jnp.concatenate(blocks, axis=0)
    rank3, _ = _top16_rows_any(cand)
    chosen = jnp.where(rank3 < PEER_TOPK, 1.0, 0.0)
    z = jnp.sum(chosen * jnp.exp(cand - cand[0:1]), axis=0, keepdims=True)
    counts = [jnp.sum(chosen[0:16], axis=0, keepdims=True)]
    for r1 in range(1, 8):
        counts.append(jnp.sum(chosen[8 + 8 * r1: 16 + 8 * r1], axis=0, keepdims=True))
    counts.append(chosen[N_CAND_ROWS - 8:])
    cnt_r = jnp.concatenate(counts, axis=0)
    cnt_k = jnp.zeros(s1.shape, F32)
    for r in range(PEER_TOPK):
        cnt_k = jnp.where(rank1 == r, cnt_r[r:r + 1], cnt_k)
    return rank2, jnp.exp(s2 - v2[0:1]), jnp.exp(s1 - v1[0:1]) / z, cnt_k


def _router_kernel(h_ref, wq_ref, k1_ref, k2_ref, rank2_ref, e2_ref, coef_ref, cnt_ref):
    hb = h_ref[...]

    def emit(h, rank2, e2, coef, cnt_k):
        rank2_ref[h] = rank2.astype(BF16)
        e2_ref[h] = e2.astype(BF16)
        coef_ref[h] = coef
        cnt_ref[h] = cnt_k

    def heads(g, carry):
        todo = []
        for i in range(ROUTER_HEADS_PER_TRIP):
            h = g * ROUTER_HEADS_PER_TRIP + i
            q_t = lax.dot_general(wq_ref[h], hb, (((1,), (1,)), ((), ())), preferred_element_type=F32)
            s1 = jnp.dot(k1_ref[...], q_t[:PEER_HALF].astype(BF16), preferred_element_type=F32)
            s2 = jnp.dot(k2_ref[...], q_t[PEER_HALF:].astype(BF16), preferred_element_type=F32)
            *stats, redo = _route_head(s1, s2)
            emit(h, *stats)
            todo.append((h, s1, s2, redo))
        for h, s1, s2, redo in todo:
            pl.when(redo > 0.0)(functools.partial(lambda h, s1, s2: emit(h, *_route_head_any(s1, s2)), h, s1, s2))
        return carry

    lax.fori_loop(0, PEER_HEADS // ROUTER_HEADS_PER_TRIP, heads, 0)


def _router_call(hb, wq_t, k1, k2):
    T, D = hb.shape
    tm = ROUTER_TILE
    o_spec = pl.BlockSpec((PEER_HEADS, PEER_NKEYS, tm), lambda t: (0, 0, t))
    shape = lambda dt: jax.ShapeDtypeStruct((PEER_HEADS, PEER_NKEYS, T), dt)
    return pl.pallas_call(
        _router_kernel,
        grid=(T // tm,),
        in_specs=[pl.BlockSpec((tm, D), lambda t: (t, 0)),
                  _resident(wq_t.shape), _resident(k1.shape), _resident(k2.shape)],
        out_specs=[o_spec] * 4,
        out_shape=[shape(BF16), shape(BF16), shape(F32), shape(F32)],
        compiler_params=_params(("parallel",)),
        name="peer_router",
    )(hb, wq_t, k1, k2)


PREP_TILE = 512


def _uvprep_kernel(u_ref, v_ref, ub_ref, vt_ref):
    ub_ref[...] = u_ref[...].astype(BF16)
    vt_ref[...] = v_ref[...].T.astype(BF16)


def _uvprep_call(u, v):
    E, D = u.shape
    te = PREP_TILE
    return pl.pallas_call(
        _uvprep_kernel,
        grid=(E // te,),
        in_specs=[pl.BlockSpec((te, D), lambda e: (e, 0)), pl.BlockSpec((te, D), lambda e: (e, 0))],
        out_specs=[pl.BlockSpec((te, D), lambda e: (e, 0)), pl.BlockSpec((D, te), lambda e: (0, e))],
        out_shape=[jax.ShapeDtypeStruct((E, D), BF16), jax.ShapeDtypeStruct((D, E), BF16)],
        compiler_params=_params(("parallel",)),
        name="peer_uvprep",
    )(u, v)


PEER_TOK_TILE = 512
PEER_EXP_TILE = 1024
KEYS_PER_STEP = PEER_EXP_TILE // PEER_NKEYS
PEER_STAGES = 3


def _peer_kernel(h_ref, u_ref, vt_ref, rank2_ref, e2_ref, coef_ref, cnt_ref, o_ref, act_ref, p_ref, *, n_exp_tiles):
    s = pl.program_id(0)
    cur = s % 2
    prev = (s + 1) % 2

    @pl.when(s == 0)
    def _():
        act_ref[1] = jnp.zeros(act_ref.shape[1:], BF16)
        p_ref[0] = jnp.zeros(p_ref.shape[1:], BF16)

    @pl.when(jnp.maximum(s - (PEER_STAGES - 1), 0) % n_exp_tiles == 0)
    def _():
        o_ref[...] = jnp.zeros(o_ref.shape, F32)

    act_ref[cur] = lax.dot_general(u_ref[...], h_ref[...], (((1,), (1,)), ((), ())),
                                   preferred_element_type=F32).astype(BF16)

    for key in range(KEYS_PER_STEP):
        gate = None
        for h in range(PEER_HEADS):
            cnt = cnt_ref[h, key:key + 1, :].astype(BF16)
            cf = coef_ref[h, key:key + 1, :].astype(BF16)
            term = jnp.where(rank2_ref[h] < cnt, e2_ref[h], jnp.zeros((), BF16)) * cf
            gate = term if gate is None else gate + term
        rows = slice(key * PEER_NKEYS, (key + 1) * PEER_NKEYS)
        p_ref[prev, rows, :] = gate * _gelu_tanh(act_ref[prev, rows, :])

    o_ref[...] += jnp.dot(vt_ref[...], p_ref[cur], preferred_element_type=F32)


def _peer_call(hb, ub, vt, rank2, e2, coef, cnt):
    T, D = hb.shape
    tm, te = PEER_TOK_TILE, PEER_EXP_TILE
    n_e = PEER_EXPERTS // te
    n_pairs = (T // tm) * n_e
    by_tile = lambda a: a.reshape(PEER_HEADS, n_e, KEYS_PER_STEP, T)

    def pair(s, lag):
        i = jnp.clip(s - lag, 0, n_pairs - 1)
        return i // n_e, i % n_e

    head_spec = pl.BlockSpec((PEER_HEADS, PEER_NKEYS, tm), lambda s: (0, 0, pair(s, 1)[0]))
    key_spec = pl.BlockSpec((PEER_HEADS, None, KEYS_PER_STEP, tm),
                            lambda s: (0, pair(s, 1)[1], 0, pair(s, 1)[0]))
    return pl.pallas_call(
        functools.partial(_peer_kernel, n_exp_tiles=n_e),
        grid=(n_pairs + PEER_STAGES - 1,),
        in_specs=[pl.BlockSpec((tm, D), lambda s: (pair(s, 0)[0], 0)),
                  pl.BlockSpec((te, D), lambda s: (pair(s, 0)[1], 0)),
                  pl.BlockSpec((D, te), lambda s: (0, pair(s, 2)[1])),
                  head_spec, head_spec, key_spec, key_spec],
        out_specs=pl.BlockSpec((D, tm), lambda s: (0, pair(s, 2)[0])),
        out_shape=jax.ShapeDtypeStruct((D, T), F32),
        scratch_shapes=[pltpu.VMEM((2, te, tm), BF16), pltpu.VMEM((2, te, tm), BF16)],
        compiler_params=_params(("arbitrary",)),
        name="peer_dense",
    )(hb, ub, vt, rank2, e2, by_tile(coef), by_tile(cnt))


FINAL_TILE = 512


def _final_kernel(h_ref, yt_ref, g_ref, b_ref, o_ref, *, depth):
    o_ref[...] = _layer_norm(_deepnorm_alpha(depth) * h_ref[...] + yt_ref[...].T, g_ref[...], b_ref[...])


def _final_call(h1, y_t, ln_g, ln_b, depth):
    T, D = h1.shape
    tm = FINAL_TILE
    return pl.pallas_call(
        functools.partial(_final_kernel, depth=depth),
        grid=(T // tm,),
        in_specs=[pl.BlockSpec((tm, D), lambda t: (t, 0)),
                  pl.BlockSpec((D, tm), lambda t: (0, t)),
                  _resident((1, D)), _resident((1, D))],
        out_specs=pl.BlockSpec((tm, D), lambda t: (t, 0)),
        out_shape=jax.ShapeDtypeStruct((T, D), F32),
        compiler_params=_params(("parallel",)),
        name="final_ln",
    )(h1, y_t, ln_g.reshape(1, D), ln_b.reshape(1, D))


def _mixer(h, positions, w_in, conv_w, conv_b, rg_w_a, rg_b_a, rg_w_x, rg_b_x, lru_lambda,
           gn_attn, gn_rec, w_out, ln1_g, ln1_b, peer_wq, peer_u, peer_v, depth):
    B, S, D = h.shape
    half = HEAD_DIM // 2
    inv = ROPE_THETA ** (-jnp.arange(half, dtype=F32) / half)
    inv_full = jnp.concatenate([inv, inv]).reshape(1, HEAD_DIM)
    w_qkv = w_in[:, :3 * ATTN_WIDTH].astype(BF16)
    wax = jnp.concatenate([rg_w_a, rg_w_x], axis=-1).astype(BF16)
    qkv_steps = B * (S // QKV_TILE)
    halves = []
    if (3 * ATTN_WIDTH) % REC_WIDTH == 0 and D % qkv_steps == 0:
        step = lambda b, s: b * (S // QKV_TILE) + s
        for i in range(2):
            col = 3 * ATTN_WIDTH // REC_WIDTH + i
            halves.append(_Rider(w_in, (D // qkv_steps, REC_WIDTH), lambda b, s, col=col: (step(b, s), col),
                                 (D, REC_WIDTH), (D // qkv_steps, REC_WIDTH), lambda b, s: (step(b, s), 0), False))
    if halves and all(r.fits for r in halves):
        q, k, v, w_x, w_g = _qkv_call(h, positions.reshape(B, S, 1), inv_full, w_qkv, halves)
    else:
        q, k, v = _qkv_call(h, positions.reshape(B, S, 1), inv_full, w_qkv)
        w_x = w_in[:, 3 * ATTN_WIDTH:3 * ATTN_WIDTH + REC_WIDTH].astype(BF16)
        w_g = w_in[:, 3 * ATTN_WIDTH + REC_WIDTH:].astype(BF16)
    rec_n = _rec_call(h, w_x, w_g, conv_w, conv_b, wax, rg_b_a, rg_b_x, lru_lambda, gn_rec)
    n_steps = B * ATTN_HEADS
    wq_rider = None
    if PEER_HEADS == ATTN_HEADS and D % B == 0:
        wq_rider = _Rider(peer_wq, (D // B, 2 * PEER_HALF), lambda b, h: (b, h),
                          (PEER_HEADS, 2 * PEER_HALF, D), (None, 2 * PEER_HALF, D // B), lambda b, h: (h, 0, b), True)
    wanted = {"w_out": _row_slices(w_out, n_steps, ATTN_HEADS, False), "wq_t": wq_rider,
              "ub": _row_slices(peer_u, n_steps, ATTN_HEADS, False),
              "vt": _row_slices(peer_v, n_steps, ATTN_HEADS, True)}
    names = [n for n, r in wanted.items() if r is not None and r.fits]
    attn, *converted = _attn_call(q, k, v, [wanted[n] for n in names])
    done = dict(zip(names, converted))
    w_out_b = done["w_out"] if "w_out" in done else w_out.astype(BF16)
    h1, h1b = _mix_call(h, attn, rec_n, w_out_b, gn_attn, ln1_g, ln1_b, depth)
    return h1, h1b, done


def _peer(h, hb, wq, keys1, keys2, u, v, done, ln2_g, ln2_b, depth):
    B, S, D = h.shape
    h2 = h.reshape(B * S, D)
    hb2 = hb.reshape(B * S, D)
    if "wq_t" in done:
        wq_t = done["wq_t"]
    else:
        wq_t = wq.reshape(D, PEER_HEADS, 2 * PEER_HALF).transpose(1, 2, 0).astype(BF16)
    rank2, e2, coef, cnt = _router_call(hb2, wq_t, keys1.astype(BF16), keys2.astype(BF16))
    if "ub" in done and "vt" in done:
        ub, vt = done["ub"], done["vt"]
    else:
        ub, vt = _uvprep_call(u, v)
    y_t = _peer_call(hb2, ub, vt, rank2, e2, coef, cnt)
    return _final_call(h2, y_t, ln2_g, ln2_b, depth).reshape(B, S, D)


def kernel(x, positions, w_in, conv_w, conv_b, rg_w_a, rg_b_a, rg_w_x, rg_b_x, lru_lambda, gn_attn, gn_rec,
           w_out, ln1_g, ln1_b, peer_wq, peer_keys1, peer_keys2, peer_u, peer_v, ln2_g, ln2_b):
    depth = w_in.shape[0]
    h = x
    for l in range(depth):
        h, hb, done = _mixer(h, positions, w_in[l], conv_w[l], conv_b[l], rg_w_a[l], rg_b_a[l], rg_w_x[l],
                             rg_b_x[l], lru_lambda[l], gn_attn[l], gn_rec[l], w_out[l], ln1_g[l], ln1_b[l],
                             peer_wq[l], peer_u[l], peer_v[l], depth)
        h = _peer(h, hb, peer_wq[l], peer_keys1[l], peer_keys2[l], peer_u[l], peer_v[l], done, ln2_g[l],
                  ln2_b[l], depth)
    return h
```

```python
import functools
import math

import jax
import jax.numpy as jnp
import numpy as np
from jax import lax
from jax.experimental import pallas as pl
from jax.experimental.pallas import tpu as pltpu

F32 = jnp.float32
BF16 = jnp.bfloat16

ATTN_HEADS = 8
HEAD_DIM = 128
ATTN_WIDTH = ATTN_HEADS * HEAD_DIM
REC_BLOCKS = 8
REC_BLOCK = 128
REC_WIDTH = REC_BLOCKS * REC_BLOCK
CONV_WIDTH = 4
LRU_C = 8.0
BAND = 128
DILATIONS = (1, 4, 16)
ROPE_THETA = 10000.0
NEG_INF = -1e30
PEER_HEADS = 8
PEER_NKEYS = 128
PEER_EXPERTS = PEER_NKEYS * PEER_NKEYS
PEER_HALF = 128
PEER_TOPK = 16
SUBLANES = 8

V7X_VMEM_LIMIT = 56 * 1024 * 1024


def _params(sem, vmem=V7X_VMEM_LIMIT):
    return pltpu.CompilerParams(dimension_semantics=sem, vmem_limit_bytes=vmem)


def _resident(shape):
    zeros = (0,) * len(shape)
    return pl.BlockSpec(shape, lambda *_: zeros, pipeline_mode=pl.Buffered(1))


def _exact(v):
    assert float(np.float32(v)) == float(v)
    return jnp.full((1, 1), v, F32)


_PI_HI = float(np.float32(math.pi))
_PI_MID = float(np.float32(math.pi - _PI_HI))
_PI_LO = float(np.float32(math.pi - _PI_HI - _PI_MID))


def _ln_eps():
    return _exact(1.0) / _exact(1e5)


def _gelu_tanh(x):
    pi = _exact(_PI_HI) + _exact(_PI_MID) + _exact(_PI_LO)
    c = jnp.sqrt(_exact(2.0) / pi).astype(x.dtype)
    k = (_exact(44715.0) / _exact(1e6)).astype(x.dtype)
    return 0.5 * x * (1.0 + jnp.tanh(c * (x + k * (x * x * x))))


def _sigmoid(x):
    return 1.0 / (1.0 + jnp.exp(-x))


def _rms_scale(x):
    return lax.rsqrt(jnp.mean(x * x, axis=-1, keepdims=True) + _ln_eps())


def _layer_norm(z, g, b):
    mu = jnp.mean(z, axis=-1, keepdims=True)
    zc = z - mu
    var = jnp.mean(zc * zc, axis=-1, keepdims=True)
    return zc * lax.rsqrt(var + _ln_eps()) * g + b


def _deepnorm_alpha(depth):
    return jnp.sqrt(jnp.sqrt(_exact(2.0 * depth)))


QKV_TILE = 512


def _qkv_kernel(x_ref, pos_ref, inv_ref, w_ref, *rest, n_riders):
    srcs, (q_ref, k_ref, v_ref), dsts = rest[:n_riders], rest[n_riders:n_riders + 3], rest[n_riders + 3:]
    for src, dst in zip(srcs, dsts):
        dst[...] = src[...].astype(BF16)
    proj = jnp.dot(x_ref[0].astype(BF16), w_ref[...], preferred_element_type=F32)
    ang = pos_ref[0].astype(F32) * inv_ref[...]
    lane = lax.broadcasted_iota(jnp.int32, ang.shape, 1)
    cos = jnp.cos(ang)
    sin = jnp.where(lane < HEAD_DIM // 2, -jnp.sin(ang), jnp.sin(ang))
    scale = lax.rsqrt(_exact(float(HEAD_DIM)))
    for h in range(ATTN_HEADS):
        cols = slice(h * HEAD_DIM, (h + 1) * HEAD_DIM)
        qh = proj[:, cols]
        kh = proj[:, ATTN_WIDTH + h * HEAD_DIM: ATTN_WIDTH + (h + 1) * HEAD_DIM]
        q_ref[0, :, cols] = (qh * cos + pltpu.roll(qh, HEAD_DIM // 2, 1) * sin) * scale
        k_ref[0, :, cols] = kh * cos + pltpu.roll(kh, HEAD_DIM // 2, 1) * sin
    v_ref[0] = proj[:, 2 * ATTN_WIDTH:]


def _qkv_call(x, pos, inv_full, w_qkv, riders=()):
    B, S, D = x.shape
    tm = QKV_TILE
    spec = pl.BlockSpec((1, tm, ATTN_WIDTH), lambda b, s: (b, s, 0))
    shape = jax.ShapeDtypeStruct((B, S, ATTN_WIDTH), F32)
    assert not any(r.transposed for r in riders)
    return pl.pallas_call(
        functools.partial(_qkv_kernel, n_riders=len(riders)),
        grid=(B, S // tm),
        in_specs=[pl.BlockSpec((1, tm, D), lambda b, s: (b, s, 0)),
                  pl.BlockSpec((1, tm, 1), lambda b, s: (b, s, 0)),
                  _resident((1, HEAD_DIM)),
                  _resident(w_qkv.shape)] + [r.in_spec for r in riders],
        out_specs=[spec, spec, spec] + [r.out_spec for r in riders],
        out_shape=[shape, shape, shape] + [r.out_shape for r in riders],
        compiler_params=_params(("parallel", "parallel")),
        name="qkv_rope",
    )(x, pos, inv_full, w_qkv, *[r.src for r in riders])


REC_TILE = 512


def _rec_kernel(x_ref, wx_ref, wg_ref, cw_ref, cb_ref, wax_ref, ba_ref, bx_ref, lam_ref, g_ref, o_ref,
                ext_ref, a_ref, b_ref, h_ref, hc_ref):
    tm = REC_TILE

    @pl.when(pl.program_id(1) == 0)
    def _():
        ext_ref[0:SUBLANES, :] = jnp.zeros((SUBLANES, REC_WIDTH), F32)
        hc_ref[...] = jnp.zeros((SUBLANES, REC_WIDTH), F32)

    xb = x_ref[0].astype(BF16)
    xr = jnp.dot(xb, wx_ref[...], preferred_element_type=F32)
    gr = jnp.dot(xb, wg_ref[...], preferred_element_type=F32)
    ext_ref[SUBLANES:2 * SUBLANES, :] = xr[:SUBLANES, :]
    xc = cb_ref[...] + cw_ref[CONV_WIDTH - 1:CONV_WIDTH, :] * xr
    xc_head = xc[:SUBLANES, :]
    for j in range(CONV_WIDTH - 1):
        back = CONV_WIDTH - 1 - j
        xc = xc + cw_ref[j:j + 1, :] * pltpu.roll(xr, back, 0)
        xc_head = xc_head + cw_ref[j:j + 1, :] * ext_ref[pl.ds(SUBLANES - back, SUBLANES), :]
    xc = jnp.concatenate([xc_head, xc[SUBLANES:, :]], axis=0)
    ext_ref[0:SUBLANES, :] = xr[tm - SUBLANES:, :]

    rs, is_ = [], []
    for n in range(REC_BLOCKS):
        xb = xc[:, n * REC_BLOCK:(n + 1) * REC_BLOCK].astype(BF16)
        gts = jnp.dot(xb, wax_ref[n], preferred_element_type=F32)
        rs.append(gts[:, :REC_BLOCK])
        is_.append(gts[:, REC_BLOCK:])
    r = _sigmoid(jnp.concatenate(rs, axis=1) + ba_ref[...])
    ig = _sigmoid(jnp.concatenate(is_, axis=1) + bx_ref[...])
    z = -lam_ref[...]
    softplus = jnp.maximum(z, 0.0) + jnp.log(1.0 + jnp.exp(-jnp.abs(z)))
    log_a = (-LRU_C) * r * softplus
    a = jnp.exp(log_a)
    gap = 1.0 - jnp.exp(2.0 * log_a)
    bt = jnp.where(gap > 0.0, gap * lax.rsqrt(gap), 0.0) * (ig * xc)

    r8 = lax.broadcasted_iota(jnp.int32, a.shape, 0) & (SUBLANES - 1)
    for s in (1, 2, 4):
        a_sh = pltpu.roll(a, s, 0)
        b_sh = pltpu.roll(bt, s, 0)
        m = r8 >= s
        bt = jnp.where(m, a * b_sh + bt, bt)
        a = jnp.where(m, a * a_sh, a)
    a_ref[...] = a
    b_ref[...] = bt

    def group(g, hc):
        rows = pl.ds(pl.multiple_of(g * SUBLANES, SUBLANES), SUBLANES)
        h = a_ref[rows, :] * hc + b_ref[rows, :]
        h_ref[rows, :] = h
        return jnp.broadcast_to(h[SUBLANES - 1:SUBLANES, :], (SUBLANES, REC_WIDTH))

    hc_ref[...] = lax.fori_loop(0, tm // SUBLANES, group, hc_ref[...], unroll=8)

    rec = h_ref[...] * _gelu_tanh(gr)
    o_ref[0] = (rec * _rms_scale(rec) * g_ref[...]).astype(BF16)


def _rec_call(x, w_x, w_g, conv_w, conv_b, wax, b_a, b_x, lam, gn_rec):
    B, S, D = x.shape
    tm = REC_TILE
    row = lambda a: a.reshape(1, REC_WIDTH)
    return pl.pallas_call(
        _rec_kernel,
        grid=(B, S // tm),
        in_specs=[pl.BlockSpec((1, tm, D), lambda b, s: (b, s, 0)),
                  _resident(w_x.shape), _resident(w_g.shape),
                  _resident((CONV_WIDTH, REC_WIDTH)),
                  _resident((1, REC_WIDTH)),
                  _resident(wax.shape),
                  _resident((1, REC_WIDTH)), _resident((1, REC_WIDTH)),
                  _resident((1, REC_WIDTH)), _resident((1, REC_WIDTH))],
        out_specs=pl.BlockSpec((1, tm, REC_WIDTH), lambda b, s: (b, s, 0)),
        out_shape=jax.ShapeDtypeStruct((B, S, REC_WIDTH), BF16),
        scratch_shapes=[pltpu.VMEM((2 * SUBLANES, REC_WIDTH), F32),
                        pltpu.VMEM((tm, REC_WIDTH), F32),
                        pltpu.VMEM((tm, REC_WIDTH), F32),
                        pltpu.VMEM((tm, REC_WIDTH), F32),
                        pltpu.VMEM((SUBLANES, REC_WIDTH), F32)],
        compiler_params=_params(("parallel", "arbitrary")),
        name="rec_branch",
    )(x, w_x, w_g, conv_w, row(conv_b), wax, row(b_a), row(b_x), row(lam), row(gn_rec))


ATTN_UNROLL = 16


def _band_block(q, k, v, valid):
    s = lax.dot_general(q.astype(BF16), k.astype(BF16), (((1,), (1,)), ((), ())), preferred_element_type=F32)
    s = jnp.where(valid, s, NEG_INF)
    m = jnp.max(s, axis=-1, keepdims=True)
    p = jnp.exp(s - m)
    l = jnp.sum(p, axis=-1, keepdims=True)
    acc = jnp.dot(p.astype(BF16), v.astype(BF16), preferred_element_type=F32)
    return acc, m, l


def _attn_kernel(q_ref, k_ref, v_ref, *rest, seq, transposed):
    n = len(transposed)
    srcs, o_ref, dsts, stats = rest[:n], rest[n], rest[n + 1:2 * n + 1], rest[2 * n + 1:]
    for src, dst, flip in zip(srcs, dsts, transposed):
        dst[...] = (src[...].T if flip else src[...]).astype(BF16)
    wide = (BAND, HEAD_DIM)

    for p, dil in enumerate(DILATIONS):
        acc_ref, m_ref, l_ref = stats[3 * p: 3 * p + 3]
        n_blk = seq // (dil * BAND)

        def block(idx, carry, dil=dil, n_blk=n_blk, acc_ref=acc_ref, m_ref=m_ref, l_ref=l_ref):
            c = idx // n_blk
            j = idx - c * n_blk
            k0 = jnp.maximum(j - 1, 0) * BAND
            n_keys = 2 * BAND if n_blk > 1 else BAND
            if dil == 1:
                rows = pl.ds(pl.multiple_of(j * BAND, BAND), BAND)
                krows = pl.ds(pl.multiple_of(k0, BAND), n_keys)
            else:
                rows = pl.ds(dil * j * BAND + c, BAND, stride=dil)
                krows = pl.ds(dil * k0 + c, n_keys, stride=dil)
            qi = lax.broadcasted_iota(jnp.int32, (BAND, n_keys), 0)
            kj = lax.broadcasted_iota(jnp.int32, (BAND, n_keys), 1)
            dist = (j * BAND + qi) - (k0 + kj)
            acc, m, l = _band_block(q_ref[0, rows, :], k_ref[0, krows, :], v_ref[0, krows, :],
                                    (dist >= 0) & (dist <= BAND))
            acc_ref[rows, :] = acc
            m_ref[rows, :] = jnp.broadcast_to(m, wide)
            l_ref[rows, :] = jnp.broadcast_to(l, wide)
            return carry

        lax.fori_loop(0, seq // BAND, block, 0, unroll=ATTN_UNROLL)

    def merge(j, carry):
        rows = pl.ds(pl.multiple_of(j * BAND, BAND), BAND)
        ms = [stats[3 * p + 1][rows, :] for p in range(3)]
        mx = jnp.maximum(jnp.maximum(ms[0], ms[1]), ms[2])
        ws = [jnp.exp(mm - mx) for mm in ms]
        num = sum(ws[p] * stats[3 * p][rows, :] for p in range(3))
        den = sum(ws[p] * stats[3 * p + 2][rows, :] for p in range(3))
        o_ref[0, rows, :] = num / den
        return carry

    lax.fori_loop(0, seq // BAND, merge, 0, unroll=ATTN_UNROLL)


ATTN_RIDER_BYTES_MAX = 4 * 1024 * 1024


class _Rider:
    def __init__(self, src, in_block, in_index, out_shape, out_block, out_index, transposed):
        self.src, self.transposed = src, transposed
        self.in_spec = pl.BlockSpec(in_block, in_index)
        self.out_spec = pl.BlockSpec(out_block, out_index)
        self.out_shape = jax.ShapeDtypeStruct(out_shape, BF16)
        rows, cols = [d for d in in_block if d is not None]
        self.fits = rows % 16 == 0 and cols % HEAD_DIM == 0 and rows * cols * 4 <= ATTN_RIDER_BYTES_MAX
        if transposed:
            self.fits = self.fits and rows % HEAD_DIM == 0


def _row_slices(w, n_steps, n_heads, transposed):
    rows, cols = w.shape
    r = rows // n_steps
    step = lambda b, h: b * n_heads + h
    if r * n_steps != rows:
        return None
    if transposed:
        return _Rider(w, (r, cols), lambda b, h: (step(b, h), 0), (cols, rows), (cols, r),
                      lambda b, h: (0, step(b, h)), True)
    return _Rider(w, (r, cols), lambda b, h: (step(b, h), 0), (rows, cols), (r, cols),
                  lambda b, h: (step(b, h), 0), False)


def _attn_call(q, k, v, riders=()):
    B, S, _ = q.shape
    assert S % (max(DILATIONS) * BAND) == 0
    spec = pl.BlockSpec((1, S, HEAD_DIM), lambda b, h: (b, 0, h))
    stat = pltpu.VMEM((S, HEAD_DIM), F32)
    outs = pl.pallas_call(
        functools.partial(_attn_kernel, seq=S, transposed=tuple(r.transposed for r in riders)),
        grid=(B, ATTN_HEADS),
        in_specs=[spec] * 3 + [r.in_spec for r in riders],
        out_specs=[spec] + [r.out_spec for r in riders],
        out_shape=[jax.ShapeDtypeStruct((B, S, ATTN_WIDTH), F32)] + [r.out_shape for r in riders],
        scratch_shapes=[stat] * 9,
        compiler_params=_params(("parallel", "parallel")),
        name="dilated_attn",
    )(q, k, v, *[r.src for r in riders])
    return tuple(outs)


MIX_TILE = 512


def _mix_kernel(x_ref, at_ref, rc_ref, wo_ref, ga_ref, g_ref, b_ref, o_ref, ob_ref, *, depth):
    at = at_ref[0]
    at_n = (at * _rms_scale(at) * ga_ref[...]).astype(BF16)
    y = jnp.dot(at_n, wo_ref[:ATTN_WIDTH, :], preferred_element_type=F32)
    y = y + jnp.dot(rc_ref[0], wo_ref[ATTN_WIDTH:, :], preferred_element_type=F32)
    h = _layer_norm(_deepnorm_alpha(depth) * x_ref[0] + y, g_ref[...], b_ref[...])
    o_ref[0] = h
    ob_ref[0] = h.astype(BF16)


def _mix_call(x, attn, rec_n, w_out, gn_attn, ln_g, ln_b, depth):
    B, S, D = x.shape
    tm = MIX_TILE
    row = lambda a, n: a.reshape(1, n)
    tile = lambda n: pl.BlockSpec((1, tm, n), lambda b, s: (b, s, 0))
    return pl.pallas_call(
        functools.partial(_mix_kernel, depth=depth),
        grid=(B, S // tm),
        in_specs=[tile(D), tile(ATTN_WIDTH), tile(REC_WIDTH),
                  _resident(w_out.shape),
                  _resident((1, ATTN_WIDTH)), _resident((1, D)), _resident((1, D))],
        out_specs=[tile(D), tile(D)],
        out_shape=[jax.ShapeDtypeStruct((B, S, D), F32), jax.ShapeDtypeStruct((B, S, D), BF16)],
        compiler_params=_params(("parallel", "parallel")),
        name="mix_out_ln",
    )(x, attn, rec_n, w_out, row(gn_attn, ATTN_WIDTH), row(ln_g, D), row(ln_b, D))


ROUTER_TILE = 1024
N_CAND_ROWS = 16 + 7 * 8 + 8
ROUTER_HEADS_PER_TRIP = 1


def _sort_network(n):
    def merge(lo, hi, r):
        step = r * 2
        if step < hi - lo:
            yield from merge(lo, hi, step)
            yield from merge(lo + r, hi, step)
            yield from ((i, i + r) for i in range(lo + r, hi - r, step))
        else:
            yield (lo, lo + r)

    def sort(lo, hi):
        if hi - lo >= 1:
            mid = lo + (hi - lo) // 2
            yield from sort(lo, mid)
            yield from sort(mid + 1, hi)
            yield from merge(lo, hi, 1)

    return tuple(sort(0, n - 1))


_SORT16 = _sort_network(PEER_TOPK)
_BITONIC16 = tuple((i, i + d) for d in (8, 4, 2, 1) for i in range(PEER_TOPK) if i & d == 0)


def _exchange(x, pairs):
    for i, j in pairs:
        a, b = x[i], x[j]
        if a is None:
            x[i], x[j] = b, None
        elif b is not None:
            x[i], x[j] = jnp.maximum(a, b), jnp.minimum(a, b)
    return x


def _top16_sorted(blocks):
    x = _exchange(list(blocks) + [None] * (PEER_TOPK - len(blocks)), _SORT16)
    for shift in (4, 2, 1):
        partner = [None if v is None else pltpu.roll(v, shift, 0) for v in x]
        merged = []
        for i in range(PEER_TOPK):
            a, b = x[i], partner[PEER_TOPK - 1 - i]
            merged.append(b if a is None else a if b is None else jnp.maximum(a, b))
        x = _exchange(merged, _BITONIC16)
    return x


def _sublane_blocks(a):
    return [a[i * SUBLANES:(i + 1) * SUBLANES] for i in range(a.shape[0] // SUBLANES)]


def _count(flags):
    total = sum(jnp.where(f, 1.0, 0.0) for f in flags)
    return jnp.sum(total, axis=0, keepdims=True)


def _route_head(s1, s2):
    b1, b2 = _sublane_blocks(s1), _sublane_blocks(s2)
    v1, v2 = _top16_sorted(b1), _top16_sorted(b2)
    rows = (v1[0].shape[0], v1[0].shape[1])
    sub = lax.broadcasted_iota(jnp.int32, rows, 0)

    def on_sublanes(vals):
        out = vals[0]
        for j in range(1, SUBLANES):
            out = jnp.where(sub == j, vals[j], out)
        return out

    v2_lo, v2_hi, v1_hi = on_sublanes(v2[:8]), on_sublanes(v2[8:]), on_sublanes(v1[8:])
    cand = [v1[0] + v2_lo, v1[0] + v2_hi] + [v1[r] + v2_lo for r in range(1, 8)] + [v1_hi + v2[0]]
    best = _top16_sorted(cand)
    chosen = [c >= best[PEER_TOPK - 1] for c in cand]
    picked = [jnp.where(f, 1.0, 0.0) for f in chosen]
    z = jnp.sum(sum(p * jnp.exp(c - best[0]) for p, c in zip(picked, cand)), axis=0, keepdims=True)
    cnt_r = [jnp.sum(picked[0] + picked[1], axis=0, keepdims=True)]
    cnt_r += [jnp.sum(picked[1 + r], axis=0, keepdims=True) for r in range(1, 8)]
    cnt_r += [picked[9][j:j + 1] for j in range(SUBLANES)]

    cnt_r = [jnp.broadcast_to(c, rows) for c in cnt_r]
    rank2, cnt_k, coef, e2 = [], [], [], []
    inv_z = 1.0 / z
    for blk1, blk2 in zip(b1, b2):
        r2 = jnp.full(rows, float(PEER_TOPK), F32)
        ck = jnp.zeros(rows, F32)
        for r in range(PEER_TOPK - 1, -1, -1):
            r2 = jnp.where(blk2 >= v2[r], float(r), r2)
            ck = jnp.where(blk1 == v1[r], cnt_r[r], ck)
        rank2.append(r2)
        cnt_k.append(ck)
        coef.append(jnp.exp(blk1 - v1[0]) * inv_z)
        e2.append(jnp.exp(blk2 - v2[0]))

    strict = [v[r] > v[r + 1] for v in (v1, v2) for r in range(PEER_TOPK - 1)]
    ok = _count(strict) == float(SUBLANES * len(strict))
    for blocks, vals in ((b1, v1), (b2, v2), (cand, best)):
        ok = ok & (_count([b >= vals[PEER_TOPK - 1] for b in blocks]) == float(PEER_TOPK))
    redo = jnp.max(jnp.where(ok, 0.0, 1.0))
    cat = lambda parts: jnp.concatenate(parts, axis=0)
    return cat(rank2), cat(e2), cat(coef), cat(cnt_k), redo


def _top16_rows_any(s):
    idx = lax.broadcasted_iota(jnp.int32, s.shape, 0)
    rank = jnp.full(s.shape, float(PEER_TOPK), F32)
    vals = []
    for r in range(PEER_TOPK):
        m = jnp.max(s, axis=0, keepdims=True)
        sel = idx == jnp.min(jnp.where(s == m, idx, s.shape[0]), axis=0, keepdims=True)
        rank = jnp.where(sel, float(r), rank)
        s = jnp.where(sel, -jnp.inf, s)
        vals.append(m)
    return rank, jnp.concatenate(vals, axis=0)


def _route_head_any(s1, s2):
    rank1, v1 = _top16_rows_any(s1)
    rank2, v2 = _top16_rows_any(s2)
    blocks = [v1[0:1] + v2]
    for r1 in range(1, 8):
        blocks.append(v1[r1:r1 + 1] + v2[0:8])
    blocks.append(v1[8:16] + v2[0:1])
    cand = jnp.concatenate(blocks, axis=0)
    rank3, _ = _top16_rows_any(cand)
    chosen = jnp.where(rank3 < PEER_TOPK, 1.0, 0.0)
    z = jnp.sum(chosen * jnp.exp(cand - cand[0:1]), axis=0, keepdims=True)
    counts = [jnp.sum(chosen[0:16], axis=0, keepdims=True)]
    for r1 in range(1, 8):
        counts.append(jnp.sum(chosen[8 + 8 * r1: 16 + 8 * r1], axis=0, keepdims=True))
    counts.append(chosen[N_CAND_ROWS - 8:])
    cnt_r = jnp.concatenate(counts, axis=0)
    cnt_k = jnp.zeros(s1.shape, F32)
    for r in range(PEER_TOPK):
        cnt_k = jnp.where(rank1 == r, cnt_r[r:r + 1], cnt_k)
    return rank2, jnp.exp(s2 - v2[0:1]), jnp.exp(s1 - v1[0:1]) / z, cnt_k


def _router_kernel(h_ref, wq_ref, k1_ref, k2_ref, rank2_ref, e2_ref, coef_ref, cnt_ref):
    hb = h_ref[...]

    def emit(h, rank2, e2, coef, cnt_k):
        rank2_ref[h] = rank2.astype(BF16)
        e2_ref[h] = e2.astype(BF16)
        coef_ref[h] = coef
        cnt_ref[h] = cnt_k

    def heads(g, carry):
        todo = []
        for i in range(ROUTER_HEADS_PER_TRIP):
            h = g * ROUTER_HEADS_PER_TRIP + i
            q_t = lax.dot_general(wq_ref[h], hb, (((1,), (1,)), ((), ())), preferred_element_type=F32)
            s1 = jnp.dot(k1_ref[...], q_t[:PEER_HALF].astype(BF16), preferred_element_type=F32)
            s2 = jnp.dot(k2_ref[...], q_t[PEER_HALF:].astype(BF16), preferred_element_type=F32)
            *stats, redo = _route_head(s1, s2)
            emit(h, *stats)
            todo.append((h, s1, s2, redo))
        for h, s1, s2, redo in todo:
            pl.when(redo > 0.0)(functools.partial(lambda h, s1, s2: emit(h, *_route_head_any(s1, s2)), h, s1, s2))
        return carry

    lax.fori_loop(0, PEER_HEADS // ROUTER_HEADS_PER_TRIP, heads, 0)


def _router_call(hb, wq_t, k1, k2):
    T, D = hb.shape
    tm = ROUTER_TILE
    o_spec = pl.BlockSpec((PEER_HEADS, PEER_NKEYS, tm), lambda t: (0, 0, t))
    shape = lambda dt: jax.ShapeDtypeStruct((PEER_HEADS, PEER_NKEYS, T), dt)
    return pl.pallas_call(
        _router_kernel,
        grid=(T // tm,),
        in_specs=[pl.BlockSpec((tm, D), lambda t: (t, 0)),
                  _resident(wq_t.shape), _resident(k1.shape), _resident(k2.shape)],
        out_specs=[o_spec] * 4,
        out_shape=[shape(BF16), shape(BF16), shape(F32), shape(F32)],
        compiler_params=_params(("parallel",)),
        name="peer_router",
    )(hb, wq_t, k1, k2)


PREP_TILE = 512


def _uvprep_kernel(u_ref, v_ref, ub_ref, vt_ref):
    ub_ref[...] = u_ref[...].astype(BF16)
    vt_ref[...] = v_ref[...].T.astype(BF16)


def _uvprep_call(u, v):
    E, D = u.shape
    te = PREP_TILE
    return pl.pallas_call(
        _uvprep_kernel,
        grid=(E // te,),
        in_specs=[pl.BlockSpec((te, D), lambda e: (e, 0)), pl.BlockSpec((te, D), lambda e: (e, 0))],
        out_specs=[pl.BlockSpec((te, D), lambda e: (e, 0)), pl.BlockSpec((D, te), lambda e: (0, e))],
        out_shape=[jax.ShapeDtypeStruct((E, D), BF16), jax.ShapeDtypeStruct((D, E), BF16)],
        compiler_params=_params(("parallel",)),
        name="peer_uvprep",
    )(u, v)


PEER_TOK_TILE = 512
PEER_EXP_TILE = 1024
KEYS_PER_STEP = PEER_EXP_TILE // PEER_NKEYS
PEER_STAGES = 3


def _peer_kernel(h_ref, u_ref, vt_ref, rank2_ref, e2_ref, coef_ref, cnt_ref, o_ref, act_ref, p_ref, *, n_exp_tiles):
    s = pl.program_id(0)
    cur = s % 2
    prev = (s + 1) % 2

    @pl.when(s == 0)
    def _():
        act_ref[1] = jnp.zeros(act_ref.shape[1:], BF16)
        p_ref[0] = jnp.zeros(p_ref.shape[1:], BF16)

    @pl.when(jnp.maximum(s - (PEER_STAGES - 1), 0) % n_exp_tiles == 0)
    def _():
        o_ref[...] = jnp.zeros(o_ref.shape, F32)

    act_ref[cur] = lax.dot_general(u_ref[...], h_ref[...], (((1,), (1,)), ((), ())),
                                   preferred_element_type=F32).astype(BF16)

    cnt_all = [cnt_ref[h].astype(BF16) for h in range(PEER_HEADS)]
    cf_all = [coef_ref[h].astype(BF16) for h in range(PEER_HEADS)]
    for key in range(KEYS_PER_STEP):
        gate = None
        for h in range(PEER_HEADS):
            cnt = cnt_all[h][key:key + 1, :]
            cf = cf_all[h][key:key + 1, :]
            term = jnp.where(rank2_ref[h] < cnt, e2_ref[h], jnp.zeros((), BF16)) * cf
            gate = term if gate is None else gate + term
        rows = slice(key * PEER_NKEYS, (key + 1) * PEER_NKEYS)
        p_ref[prev, rows, :] = gate * _gelu_tanh(act_ref[prev, rows, :])

    o_ref[...] += jnp.dot(vt_ref[...], p_ref[cur], preferred_element_type=F32)


def _peer_call(hb, ub, vt, rank2, e2, coef, cnt):
    T, D = hb.shape
    tm, te = PEER_TOK_TILE, PEER_EXP_TILE
    n_e = PEER_EXPERTS // te
    n_pairs = (T // tm) * n_e
    by_tile = lambda a: a.reshape(PEER_HEADS, n_e, KEYS_PER_STEP, T)

    def pair(s, lag):
        i = jnp.clip(s - lag, 0, n_pairs - 1)
        return i // n_e, i % n_e

    head_spec = pl.BlockSpec((PEER_HEADS, PEER_NKEYS, tm), lambda s: (0, 0, pair(s, 1)[0]))
    key_spec = pl.BlockSpec((PEER_HEADS, None, KEYS_PER_STEP, tm),
                            lambda s: (0, pair(s, 1)[1], 0, pair(s, 1)[0]))
    return pl.pallas_call(
        functools.partial(_peer_kernel, n_exp_tiles=n_e),
        grid=(n_pairs + PEER_STAGES - 1,),
        in_specs=[pl.BlockSpec((tm, D), lambda s: (pair(s, 0)[0], 0)),
                  pl.BlockSpec((te, D), lambda s: (pair(s, 0)[1], 0)),
                  pl.BlockSpec((D, te), lambda s: (0, pair(s, 2)[1])),
                  head_spec, head_spec, key_spec, key_spec],
        out_specs=pl.BlockSpec((D, tm), lambda s: (0, pair(s, 2)[0])),
        out_shape=jax.ShapeDtypeStruct((D, T), F32),
        scratch_shapes=[pltpu.VMEM((2, te, tm), BF16), pltpu.VMEM((2, te, tm), BF16)],
        compiler_params=_params(("arbitrary",)),
        name="peer_dense",
    )(hb, ub, vt, rank2, e2, by_tile(coef), by_tile(cnt))


FINAL_TILE = 512


def _final_kernel(h_ref, yt_ref, g_ref, b_ref, o_ref, *, depth):
    o_ref[...] = _layer_norm(_deepnorm_alpha(depth) * h_ref[...] + yt_ref[...].T, g_ref[...], b_ref[...])


def _final_call(h1, y_t, ln_g, ln_b, depth):
    T, D = h1.shape
    tm = FINAL_TILE
    return pl.pallas_call(
        functools.partial(_final_kernel, depth=depth),
        grid=(T // tm,),
        in_specs=[pl.BlockSpec((tm, D), lambda t: (t, 0)),
                  pl.BlockSpec((D, tm), lambda t: (0, t)),
                  _resident((1, D)), _resident((1, D))],
        out_specs=pl.BlockSpec((tm, D), lambda t: (t, 0)),
        out_shape=jax.ShapeDtypeStruct((T, D), F32),
        compiler_params=_params(("parallel",)),
        name="final_ln",
    )(h1, y_t, ln_g.reshape(1, D), ln_b.reshape(1, D))


def _mixer(h, positions, w_in, conv_w, conv_b, rg_w_a, rg_b_a, rg_w_x, rg_b_x, lru_lambda,
           gn_attn, gn_rec, w_out, ln1_g, ln1_b, peer_wq, peer_u, peer_v, depth):
    B, S, D = h.shape
    half = HEAD_DIM // 2
    inv = ROPE_THETA ** (-jnp.arange(half, dtype=F32) / half)
    inv_full = jnp.concatenate([inv, inv]).reshape(1, HEAD_DIM)
    w_qkv = w_in[:, :3 * ATTN_WIDTH].astype(BF16)
    wax = jnp.concatenate([rg_w_a, rg_w_x], axis=-1).astype(BF16)
    qkv_steps = B * (S // QKV_TILE)
    halves = []
    if (3 * ATTN_WIDTH) % REC_WIDTH == 0 and D % qkv_steps == 0:
        step = lambda b, s: b * (S // QKV_TILE) + s
        for i in range(2):
            col = 3 * ATTN_WIDTH // REC_WIDTH + i
            halves.append(_Rider(w_in, (D // qkv_steps, REC_WIDTH), lambda b, s, col=col: (step(b, s), col),
                                 (D, REC_WIDTH), (D // qkv_steps, REC_WIDTH), lambda b, s: (step(b, s), 0), False))
    if halves and all(r.fits for r in halves):
        q, k, v, w_x, w_g = _qkv_call(h, positions.reshape(B, S, 1), inv_full, w_qkv, halves)
    else:
        q, k, v = _qkv_call(h, positions.reshape(B, S, 1), inv_full, w_qkv)
        w_x = w_in[:, 3 * ATTN_WIDTH:3 * ATTN_WIDTH + REC_WIDTH].astype(BF16)
        w_g = w_in[:, 3 * ATTN_WIDTH + REC_WIDTH:].astype(BF16)
    rec_n = _rec_call(h, w_x, w_g, conv_w, conv_b, wax, rg_b_a, rg_b_x, lru_lambda, gn_rec)
    n_steps = B * ATTN_HEADS
    wq_rider = None
    if PEER_HEADS == ATTN_HEADS and D % B == 0:
        wq_rider = _Rider(peer_wq, (D // B, 2 * PEER_HALF), lambda b, h: (b, h),
                          (PEER_HEADS, 2 * PEER_HALF, D), (None, 2 * PEER_HALF, D // B), lambda b, h: (h, 0, b), True)
    wanted = {"w_out": _row_slices(w_out, n_steps, ATTN_HEADS, False), "wq_t": wq_rider,
              "ub": _row_slices(peer_u, n_steps, ATTN_HEADS, False),
              "vt": _row_slices(peer_v, n_steps, ATTN_HEADS, True)}
    names = [n for n, r in wanted.items() if r is not None and r.fits]
    attn, *converted = _attn_call(q, k, v, [wanted[n] for n in names])
    done = dict(zip(names, converted))
    w_out_b = done["w_out"] if "w_out" in done else w_out.astype(BF16)
    h1, h1b = _mix_call(h, attn, rec_n, w_out_b, gn_attn, ln1_g, ln1_b, depth)
    return h1, h1b, done


def _peer(h, hb, wq, keys1, keys2, u, v, done, ln2_g, ln2_b, depth):
    B, S, D = h.shape
    h2 = h.reshape(B * S, D)
    hb2 = hb.reshape(B * S, D)
    if "wq_t" in done:
        wq_t = done["wq_t"]
    else:
        wq_t = wq.reshape(D, PEER_HEADS, 2 * PEER_HALF).transpose(1, 2, 0).astype(BF16)
    rank2, e2, coef, cnt = _router_call(hb2, wq_t, keys1.astype(BF16), keys2.astype(BF16))
    if "ub" in done and "vt" in done:
        ub, vt = done["ub"], done["vt"]
    else:
        ub, vt = _uvprep_call(u, v)
    y_t = _peer_call(hb2, ub, vt, rank2, e2, coef, cnt)
    return _final_call(h2, y_t, ln2_g, ln2_b, depth).reshape(B, S, D)


def kernel(x, positions, w_in, conv_w, conv_b, rg_w_a, rg_b_a, rg_w_x, rg_b_x, lru_lambda, gn_attn, gn_rec,
           w_out, ln1_g, ln1_b, peer_wq, peer_keys1, peer_keys2, peer_u, peer_v, ln2_g, ln2_b):
    depth = w_in.shape[0]
    h = x
    for l in range(depth):
        h, hb, done = _mixer(h, positions, w_in[l], conv_w[l], conv_b[l], rg_w_a[l], rg_b_a[l], rg_w_x[l],
                             rg_b_x[l], lru_lambda[l], gn_attn[l], gn_rec[l], w_out[l], ln1_g[l], ln1_b[l],
                             peer_wq[l], peer_u[l], peer_v[l], depth)
        h = _peer(h, hb, peer_wq[l], peer_keys1[l], peer_keys2[l], peer_u[l], peer_v[l], done, ln2_g[l],
                  ln2_b[l], depth)
    return h
```
